```python
import math
import jax
import jax.numpy as jnp
from jax import lax
import numpy as np

D_MODEL = 4096
BATCH = 2
SEQ = 8192
DEPTH = 2

N_A_LAYERS = DEPTH // 2
N_B_LAYERS = DEPTH - N_A_LAYERS
N_DENSE_LAYERS = (DEPTH + 1) // 2
N_MOE_LAYERS = DEPTH // 2

RET_HEADS = 16
RET_QK_DIM = D_MODEL // RET_HEADS
RET_V_DIM = 2 * RET_QK_DIM
RET_CHUNK = 128
ROPE_BASE = 10000.0

ATT_HEADS = 32
ATT_HEAD_DIM = D_MODEL // ATT_HEADS
MOBA_BLOCK = 256
MOBA_TOPK = 3
MOBA_Q_CHUNK = 16

REL_BUCKETS = 32
REL_MAX_DISTANCE = 4096

FFN_DIM = 256 * ((8 * D_MODEL // 3 + 255) // 256)
N_EXPERTS = 8
TOP_K_EXPERTS = 2
EXPERT_DIM = D_MODEL

EPS = 1e-6

kernel_name = 'hybrid_retention_moba_yoco_moe'


def rms_norm(x, g):
    x32 = x.astype(jnp.float32)
    y = x32 * lax.rsqrt(jnp.mean(x32 * x32, axis=-1, keepdims=True) + EPS)
    return (y * g.astype(jnp.float32)).astype(x.dtype)


def rotary(x, pos):
    d = x.shape[-1]
    inv = 1.0 / (ROPE_BASE ** jnp.linspace(0.0, 1.0, d // 2, dtype=jnp.float32))
    ang = pos.astype(jnp.float32)[:, None] * inv[None, :]
    cos = jnp.cos(ang)[None, :, None, :]
    sin = jnp.sin(ang)[None, :, None, :]
    x1, x2 = jnp.split(x, 2, axis=-1)
    return jnp.concatenate([x1 * cos - x2 * sin, x1 * sin + x2 * cos], axis=-1)


def retention(xn, w_in, gn_g, w_out):
    B, S, _ = xn.shape
    H, dk, dv = RET_HEADS, RET_QK_DIM, RET_V_DIM
    f32 = jnp.float32
    proj = xn @ w_in
    q, k, v, g = jnp.split(proj, [H * dk, 2 * H * dk, 2 * H * dk + H * dv], axis=-1)
    pos = jnp.arange(S)
    q = rotary(q.reshape(B, S, H, dk).astype(f32), pos)
    k = rotary(k.reshape(B, S, H, dk).astype(f32), pos) * (dk ** -0.5)
    v = v.reshape(B, S, H, dv).astype(f32)
    log_gamma = jnp.log1p(-jnp.exp2(-5.0 - jnp.arange(H, dtype=f32)))
    C = RET_CHUNK
    n_chunks = S // C
    idx = jnp.arange(C, dtype=f32)
    diff = idx[:, None] - idx[None, :]
    decay_intra = jnp.where(diff >= 0, jnp.exp(log_gamma[:, None, None] * jnp.maximum(diff, 0.0)), 0.0)
    decay_q = jnp.exp(log_gamma[:, None] * (idx + 1.0))[:, :, None]
    decay_k = jnp.exp(log_gamma[:, None] * (C - 1.0 - idx))[:, :, None]
    decay_chunk = jnp.exp(log_gamma * C)[:, None, None]

    def to_chunks(t):
        return t.reshape(B, n_chunks, C, H, t.shape[-1]).transpose(1, 0, 3, 2, 4)

    def step(state, qkv):
        qc, kc, vc = qkv
        scores = jnp.einsum('bhid,bhjd->bhij', qc, kc) * decay_intra
        inner = jnp.einsum('bhij,bhje->bhie', scores, vc)
        cross = jnp.einsum('bhid,bhde->bhie', qc, state) * decay_q
        state = state * decay_chunk + jnp.einsum('bhjd,bhje->bhde', kc * decay_k, vc)
        return state, inner + cross

    state0 = jnp.zeros((B, H, dk, dv), f32)
    _, o = lax.scan(step, state0, (to_chunks(q), to_chunks(k), to_chunks(v)))
    o = o.transpose(1, 0, 3, 2, 4).reshape(B, S, H, dv)
    mu = jnp.mean(o, axis=-1, keepdims=True)
    var = jnp.mean(jnp.square(o - mu), axis=-1, keepdims=True)
    o = ((o - mu) * lax.rsqrt(var + EPS)).reshape(B, S, H * dv) * gn_g.astype(f32)
    y = jax.nn.silu(g.astype(f32)) * o
    return y.astype(xn.dtype) @ w_out


def rel_bucket(dist):
    n = jnp.maximum(dist, 0)
    max_exact = REL_BUCKETS // 2
    nf = jnp.maximum(n, max_exact).astype(jnp.float32)
    large = max_exact + (jnp.log(nf / max_exact) / math.log(REL_MAX_DISTANCE / max_exact)
                         * (REL_BUCKETS - max_exact)).astype(jnp.int32)
    large = jnp.minimum(large, REL_BUCKETS - 1)
    return jnp.where(n < max_exact, n, large)


def shared_kv(h, kv_norm_g, kv_w, k_norm_g):
    B, S, _ = h.shape
    H, dh = ATT_HEADS, ATT_HEAD_DIM
    kv = rms_norm(h, kv_norm_g) @ kv_w
    k, v = jnp.split(kv, 2, axis=-1)
    k = rms_norm(k.reshape(B, S, H, dh), k_norm_g)
    v = v.reshape(B, S, H, dh)
    n_blocks = -(-S // MOBA_BLOCK)
    pad = n_blocks * MOBA_BLOCK - S
    padw = ((0, 0), (0, pad), (0, 0), (0, 0))
    k_blk = jnp.pad(k, padw).transpose(0, 2, 1, 3).reshape(B, H, n_blocks, MOBA_BLOCK, dh)
    v_blk = jnp.pad(v, padw).transpose(0, 2, 1, 3).reshape(B, H, n_blocks, MOBA_BLOCK, dh)
    k_mean = jnp.mean(k_blk.astype(jnp.float32), axis=3)
    return k_blk, v_blk, k_mean


def moba_attention(xn, w_q, q_norm_g, w_out, k_blk, v_blk, k_mean, rel_bias):
    B, S, _ = xn.shape
    H, dh, P = ATT_HEADS, ATT_HEAD_DIM, MOBA_BLOCK
    f32 = jnp.float32
    n_blocks = k_blk.shape[2]
    s_pad = n_blocks * P
    q = rms_norm((xn @ w_q).reshape(B, S, H, dh), q_norm_g)
    q = jnp.pad(q, ((0, 0), (0, s_pad - S), (0, 0), (0, 0)))
    QC = MOBA_Q_CHUNK
    n_qc = s_pad // QC
    q_chunks = q.reshape(B, n_qc, QC, H, dh).transpose(1, 0, 3, 2, 4)
    n_sel = min(MOBA_TOPK, n_blocks)
    scale = dh ** -0.5
    kb_flat = k_blk.reshape(B * H, n_blocks, P, dh)
    vb_flat = v_blk.reshape(B * H, n_blocks, P, dh)
    bias_hb = rel_bias.T.astype(f32)
    head_ix = jnp.arange(H)[None, :, None, None, None]
    gather_blocks = jax.vmap(lambda blocks, ix: blocks[ix])
    key_off = jnp.arange(P)

    def attend(args):
        qc, c = args
        t = c * QC + jnp.arange(QC)
        blk = (c * QC) // P
        gate = jnp.einsum('bhtd,bhnd->bhtn', qc, k_mean, preferred_element_type=f32)
        gate = jnp.where(jnp.arange(n_blocks) < blk, gate, -jnp.inf)
        _, sel = lax.top_k(gate, n_sel)
        slot_ok = jnp.arange(n_sel) < blk
        sel_flat = sel.reshape(B * H, QC * n_sel)
        k_sel = gather_blocks(kb_flat, sel_flat).reshape(B, H, QC, n_sel, P, dh)
        v_sel = gather_blocks(vb_flat, sel_flat).reshape(B, H, QC, n_sel, P, dh)
        sel_pos = sel[..., None] * P + key_off
        s_sel = jnp.einsum('bhtd,bhtkpd->bhtkp', qc, k_sel, preferred_element_type=f32) * scale
        s_sel = s_sel + bias_hb[head_ix, rel_bucket(t[:, None, None] - sel_pos)]
        s_sel = jnp.where(slot_ok[:, None], s_sel, -jnp.inf)
        k_own = lax.dynamic_index_in_dim(k_blk, blk, axis=2, keepdims=False)
        v_own = lax.dynamic_index_in_dim(v_blk, blk, axis=2, keepdims=False)
        dist_own = t[:, None] - (blk * P + key_off)[None, :]
        s_own = jnp.einsum('bhtd,bhpd->bhtp', qc, k_own, preferred_element_type=f32) * scale
        s_own = s_own + bias_hb[:, rel_bucket(dist_own)]
        s_own = jnp.where(dist_own >= 0, s_own, -jnp.inf)
        logits = jnp.concatenate([s_own, s_sel.reshape(B, H, QC, n_sel * P)], axis=-1)
        p = jax.nn.softmax(logits, axis=-1)
        p_own = p[..., :P].astype(v_own.dtype)
        p_sel = p[..., P:].reshape(B, H, QC, n_sel, P).astype(v_sel.dtype)
        return (jnp.einsum('bhtp,bhpd->bhtd', p_own, v_own)
                + jnp.einsum('bhtkp,bhtkpd->bhtd', p_sel, v_sel))

    o = lax.map(attend, (q_chunks, jnp.arange(n_qc)))
    o = o.transpose(1, 0, 3, 2, 4).reshape(B, s_pad, H * dh)[:, :S]
    return o @ w_out


def swiglu(x, w_gu, w_down):
    g, u = jnp.split(x @ w_gu, 2, axis=-1)
    return (jax.nn.silu(g) * u) @ w_down


def moe_swiglu(xn, router_w, w_gu, w_down):
    B, S, D = xn.shape
    tok = xn.reshape(B * S, D)
    logits = (tok @ router_w).astype(jnp.float32)
    top_val, top_idx = lax.top_k(logits, TOP_K_EXPERTS)
    top_w = jax.nn.softmax(top_val, axis=-1)
    gates = jnp.sum(jax.nn.one_hot(top_idx, N_EXPERTS, dtype=jnp.float32) * top_w[..., None], axis=1)
    y = jnp.zeros((B * S, D), jnp.float32)
    for e in range(N_EXPERTS):
        y = y + gates[:, e:e + 1] * swiglu(tok, w_gu[e], w_down[e]).astype(jnp.float32)
    return y.astype(xn.dtype).reshape(B, S, D)


def setup_inputs(seed: int = 0) -> dict:
    key = jax.random.key(seed)
    ks = jax.random.split(key, 20)
    f32 = jnp.float32

    def w(k, shape, fan_in):
        return jax.random.normal(k, shape, f32) * (fan_in ** -0.5)

    def gain(k, shape):
        return 1.0 + 0.02 * jax.random.normal(k, shape, f32)

    ret_in_dim = RET_HEADS * (2 * RET_QK_DIM + 2 * RET_V_DIM)
    ret_v_total = RET_HEADS * RET_V_DIM
    att_dim = ATT_HEADS * ATT_HEAD_DIM
    return {
        'x': jax.random.normal(ks[0], (BATCH, SEQ, D_MODEL), f32),
        'ret_norm_g': gain(ks[1], (N_A_LAYERS, D_MODEL)),
        'ret_w_in': w(ks[2], (N_A_LAYERS, D_MODEL, ret_in_dim), D_MODEL),
        'ret_gn_g': gain(ks[3], (N_A_LAYERS, ret_v_total)),
        'ret_w_out': w(ks[4], (N_A_LAYERS, ret_v_total, D_MODEL), ret_v_total),
        'kv_norm_g': gain(ks[5], (D_MODEL,)),
        'kv_w': w(ks[6], (D_MODEL, 2 * att_dim), D_MODEL),
        'k_norm_g': gain(ks[7], (ATT_HEAD_DIM,)),
        'rel_bias': 0.1 * jax.random.normal(ks[8], (REL_BUCKETS, ATT_HEADS), f32),
        'att_norm_g': gain(ks[9], (N_B_LAYERS, D_MODEL)),
        'att_w_q': w(ks[10], (N_B_LAYERS, D_MODEL, att_dim), D_MODEL),
        'q_norm_g': gain(ks[11], (N_B_LAYERS, ATT_HEAD_DIM)),
        'att_w_out': w(ks[12], (N_B_LAYERS, att_dim, D_MODEL), att_dim),
        'ffn_norm_g': gain(ks[13], (N_DENSE_LAYERS, D_MODEL)),
        'ffn_w_gu': w(ks[14], (N_DENSE_LAYERS, D_MODEL, 2 * FFN_DIM), D_MODEL),
        'ffn_w_down': w(ks[15], (N_DENSE_LAYERS, FFN_DIM, D_MODEL), FFN_DIM),
        'moe_norm_g': gain(ks[16], (N_MOE_LAYERS, D_MODEL)),
        'moe_router': w(ks[17], (N_MOE_LAYERS, D_MODEL, N_EXPERTS), D_MODEL),
        'moe_w_gu': w(ks[18], (N_MOE_LAYERS, N_EXPERTS, D_MODEL, 2 * EXPERT_DIM), D_MODEL),
        'moe_w_down': w(ks[19], (N_MOE_LAYERS, N_EXPERTS, EXPERT_DIM, D_MODEL), EXPERT_DIM),
    }


def reference(x, ret_norm_g, ret_w_in, ret_gn_g, ret_w_out, kv_norm_g, kv_w, k_norm_g, rel_bias,
              att_norm_g, att_w_q, q_norm_g, att_w_out, ffn_norm_g, ffn_w_gu, ffn_w_down,
              moe_norm_g, moe_router, moe_w_gu, moe_w_down):
    h = x
    kv_shared = None
    for layer in range(DEPTH):
        if layer < N_A_LAYERS:
            i = layer
            h = h + retention(rms_norm(h, ret_norm_g[i]), ret_w_in[i], ret_gn_g[i], ret_w_out[i])
        else:
            i = layer - N_A_LAYERS
            if kv_shared is None:
                kv_shared = shared_kv(h, kv_norm_g, kv_w, k_norm_g)
            k_blk, v_blk, k_mean = kv_shared
            h = h + moba_attention(rms_norm(h, att_norm_g[i]), att_w_q[i], q_norm_g[i], att_w_out[i],
                                   k_blk, v_blk, k_mean, rel_bias)
        j = layer // 2
        if layer % 2 == 0:
            h = h + swiglu(rms_norm(h, ffn_norm_g[j]), ffn_w_gu[j], ffn_w_down[j])
        else:
            h = h + moe_swiglu(rms_norm(h, moe_norm_g[j]), moe_router[j], moe_w_gu[j], moe_w_down[j])
    return h
```

```python
import functools
import math

import jax
import jax.numpy as jnp
from jax import lax
from jax.experimental import pallas as pl
from jax.experimental.pallas import tpu as pltpu

F32 = jnp.float32
BF16 = jnp.bfloat16

RET_QK_DIM = 256
RET_V_DIM = 512
ROPE_BASE = 10000.0
ATT_HEAD_DIM = 128
MOBA_BLOCK = 256
MOBA_TOPK = 3
REL_BUCKETS = 32
REL_MAX_DISTANCE = 4096
N_EXPERTS = 8
EPS = 1e-6

LANES = 128
V7X_VMEM_LIMIT_BYTES = 56 * 1024 * 1024
MASK_VALUE = -1e30

RET_CHUNK = 256
MOE_TM = 512
GATHER_ROWS = 256
COMBINE_ROWS = 128


def _cparams(sem):
    return pltpu.CompilerParams(dimension_semantics=sem, vmem_limit_bytes=V7X_VMEM_LIMIT_BYTES)


def _sigmoid(x):
    return 1.0 / (1.0 + jnp.exp(-x))


def _rmsnorm_kernel(x_ref, *refs):
    n = len(refs) // 2
    x = x_ref[...]
    y = x * lax.rsqrt(jnp.mean(x * x, axis=-1, keepdims=True) + EPS)
    for g_ref, o_ref in zip(refs[:n], refs[n:]):
        o_ref[...] = (y * g_ref[...]).astype(o_ref.dtype)


def _rmsnorm(x, gains, name):
    t, d = x.shape
    tm = min(256, t)
    n = len(gains)
    row = pl.BlockSpec((tm, d), lambda i: (i, 0))
    gspec = pl.BlockSpec((1, d), lambda i: (0, 0))
    outs = pl.pallas_call(
        _rmsnorm_kernel,
        out_shape=[jax.ShapeDtypeStruct((t, d), BF16)] * n,
        grid=(t // tm,),
        in_specs=[row] + [gspec] * n,
        out_specs=[row] * n,
        compiler_params=_cparams(("arbitrary",)),
        name=name,
    )(x, *[g.reshape(1, d).astype(F32) for g in gains])
    return outs


def _mm_kernel(x_ref, w_ref, *refs, mode, scale):
    o_ref = refs[-1]
    acc = jnp.dot(x_ref[...], w_ref[...], preferred_element_type=F32)
    if mode == "plain":
        o_ref[...] = acc.astype(o_ref.dtype)
    elif mode == "res":
        o_ref[...] = refs[0][...] + acc
    elif mode == "headnorm":
        g = refs[0][...] * scale
        for c in range(acc.shape[1] // LANES):
            a = acc[:, c * LANES:(c + 1) * LANES]
            y = a * lax.rsqrt(jnp.mean(a * a, axis=-1, keepdims=True) + EPS)
            o_ref[:, c * LANES:(c + 1) * LANES] = (y * g).astype(o_ref.dtype)
    else:
        raise ValueError(mode)


def _matmul(x, w, *, tm, tn, out_dtype, name, mode="plain", res=None, gain=None, scale=1.0, x_buffers=2):
    m, k = x.shape
    n = w.shape[1]
    tm, tn = min(tm, m), min(tn, n)
    assert m % tm == 0 and n % tn == 0, (m, n, tm, tn)
    xspec = pl.BlockSpec((tm, k), lambda i, j: (i, 0), pipeline_mode=pl.Buffered(x_buffers))
    in_specs = [xspec, pl.BlockSpec((k, tn), lambda i, j: (0, j))]
    args = [x, w]
    if mode == "res":
        in_specs.append(pl.BlockSpec((tm, tn), lambda i, j: (i, j)))
        args.append(res)
    elif mode == "headnorm":
        in_specs.append(pl.BlockSpec((1, LANES), lambda i, j: (0, 0)))
        args.append(gain.reshape(1, LANES).astype(F32))
    return pl.pallas_call(
        functools.partial(_mm_kernel, mode=mode, scale=scale),
        out_shape=jax.ShapeDtypeStruct((m, n), out_dtype),
        grid=(m // tm, n // tn),
        in_specs=in_specs,
        out_specs=pl.BlockSpec((tm, tn), lambda i, j: (i, j)),
        compiler_params=_cparams(("arbitrary", "arbitrary")),
        name=name,
    )(*args)


def _swiglu_kernel(x_ref, wg_ref, wu_ref, o_ref):
    x = x_ref[...]
    g = jnp.dot(x, wg_ref[...], preferred_element_type=F32)
    u = jnp.dot(x, wu_ref[...], preferred_element_type=F32)
    o_ref[...] = (g * _sigmoid(g) * u).astype(o_ref.dtype)


def _swiglu(x, w_gu, *, tm, tn, name):
    m, k = x.shape
    f = w_gu.shape[1] // 2
    tm, tn = min(tm, m), min(tn, f)
    assert m % tm == 0 and f % tn == 0
    nj = f // tn
    return pl.pallas_call(
        _swiglu_kernel,
        out_shape=jax.ShapeDtypeStruct((m, f), BF16),
        grid=(m // tm, nj),
        in_specs=[
            pl.BlockSpec((tm, k), lambda i, j: (i, 0)),
            pl.BlockSpec((k, tn), lambda i, j: (0, j)),
            pl.BlockSpec((k, tn), lambda i, j: (0, j + nj)),
        ],
        out_specs=pl.BlockSpec((tm, tn), lambda i, j: (i, j)),
        compiler_params=_cparams(("arbitrary", "arbitrary")),
        name=name,
    )(x, w_gu, w_gu)


def _retention_kernel(lg_ref, q_ref, k_ref, v_ref, g_ref, cos_ref, sin_ref, gn_ref, o_ref, state_ref, *, chunk):
    h = pl.program_id(1)
    c = pl.program_id(2)

    @pl.when(c == 0)
    def _():
        state_ref[...] = jnp.zeros_like(state_ref)

    lg = lg_ref[h]
    cos = cos_ref[...]
    sin = sin_ref[...]
    half = RET_QK_DIM // 2

    def rot(x):
        x1, x2 = x[:, :half], x[:, half:]
        return jnp.concatenate([x1 * cos - x2 * sin, x1 * sin + x2 * cos], axis=-1)

    q = rot(q_ref[...].astype(F32))
    k = rot(k_ref[...].astype(F32)) * (RET_QK_DIM ** -0.5)
    v = v_ref[...]

    row = lax.broadcasted_iota(jnp.int32, (chunk, 1), 0).astype(F32)
    col = lax.broadcasted_iota(jnp.int32, (1, chunk), 1).astype(F32)
    diff = row - col
    decay_intra = jnp.where(diff >= 0, jnp.exp(lg * jnp.maximum(diff, 0.0)), 0.0)
    decay_q = jnp.exp(lg * (row + 1.0))
    decay_k = jnp.exp(lg * (chunk - 1.0 - row))
    decay_chunk = jnp.exp(jnp.full((1, 1), chunk, F32) * lg)

    qb = q.astype(BF16)
    scores = lax.dot_general(qb, k.astype(BF16), (((1,), (1,)), ((), ())), preferred_element_type=F32)
    scores = scores * decay_intra
    inner = jnp.dot(scores.astype(BF16), v, preferred_element_type=F32)
    state = state_ref[...]
    cross = jnp.dot(qb, state.astype(BF16), preferred_element_type=F32) * decay_q
    kd = (k * decay_k).astype(BF16)
    state_ref[...] = state * decay_chunk + lax.dot_general(
        kd, v, (((0,), (0,)), ((), ())), preferred_element_type=F32)

    o = inner + cross
    mu = jnp.mean(o, axis=-1, keepdims=True)
    oc = o - mu
    var = jnp.mean(oc * oc, axis=-1, keepdims=True)
    on = oc * lax.rsqrt(var + EPS) * gn_ref[...]
    gate = g_ref[...].astype(F32)
    o_ref[...] = (gate * _sigmoid(gate) * on).astype(o_ref.dtype)


def _retention(proj, gn_g, b, s, heads):
    dk, dv = RET_QK_DIM, RET_V_DIM
    chunk = min(RET_CHUNK, s)
    assert s % chunk == 0
    pos = jnp.arange(s, dtype=F32)
    inv = 1.0 / (ROPE_BASE ** jnp.linspace(0.0, 1.0, dk // 2, dtype=F32))
    ang = pos[:, None] * inv[None, :]
    cos, sin = jnp.cos(ang), jnp.sin(ang)
    log_gamma = jnp.log1p(-jnp.exp2(-5.0 - jnp.arange(heads, dtype=F32)))
    k_off = heads
    v_off = (2 * heads * dk) // dv
    g_off = v_off + heads
    grid_spec = pltpu.PrefetchScalarGridSpec(
        num_scalar_prefetch=1,
        grid=(b, heads, s // chunk),
        in_specs=[
            pl.BlockSpec((None, chunk, dk), lambda bi, h, c, lg: (bi, c, h)),
            pl.BlockSpec((None, chunk, dk), lambda bi, h, c, lg: (bi, c, k_off + h)),
            pl.BlockSpec((None, chunk, dv), lambda bi, h, c, lg: (bi, c, v_off + h)),
            pl.BlockSpec((None, chunk, dv), lambda bi, h, c, lg: (bi, c, g_off + h)),
            pl.BlockSpec((chunk, dk // 2), lambda bi, h, c, lg: (c, 0)),
            pl.BlockSpec((chunk, dk // 2), lambda bi, h, c, lg: (c, 0)),
            pl.BlockSpec((1, dv), lambda bi, h, c, lg: (0, h)),
        ],
        out_specs=pl.BlockSpec((None, chunk, dv), lambda bi, h, c, lg: (bi, c, h)),
        scratch_shapes=[pltpu.VMEM((dk, dv), F32)],
    )
    return pl.pallas_call(
        functools.partial(_retention_kernel, chunk=chunk),
        out_shape=jax.ShapeDtypeStruct((b, s, heads * dv), BF16),
        grid_spec=grid_spec,
        compiler_params=_cparams(("arbitrary", "arbitrary", "arbitrary")),
        name="retention",
    )(log_gamma, proj, proj, proj, proj, cos, sin, gn_g.reshape(1, heads * dv).astype(F32))


def _rel_bucket(dist):
    n = jnp.maximum(dist, 0)
    max_exact = REL_BUCKETS // 2
    nf = jnp.maximum(n, max_exact).astype(F32)
    large = max_exact + (jnp.log(nf / max_exact) / math.log(REL_MAX_DISTANCE / max_exact)
                         * (REL_BUCKETS - max_exact)).astype(jnp.int32)
    large = jnp.minimum(large, REL_BUCKETS - 1)
    return jnp.where(n < max_exact, n, large)


def _bias_windows(rel_bias, s):
    v = jnp.arange(s + 255)
    dist = (s - 1) - v
    tab = jnp.where(dist[None, :] >= 0, rel_bias.astype(F32).T[:, _rel_bucket(dist)], 0.0)
    width = s + LANES
    return jnp.stack([tab[:, LANES - 1 - r: LANES - 1 - r + width] for r in range(LANES)], axis=1)


def _moba_kernel(q_ref, k_ref, v_ref, bias_ref, o_ref, kt_ref, km_ref, m_ref, l_ref, acc_ref, *, seq):
    blk = MOBA_BLOCK
    dh = ATT_HEAD_DIM
    nb = seq // blk
    i = pl.program_id(2)

    @pl.when(i == 0)
    def _():
        km_ref[...] = jnp.zeros_like(km_ref)
        sub = lax.broadcasted_iota(jnp.int32, (LANES, blk), 0)

        def fill(n, carry):
            start = pl.multiple_of(n * blk, blk)
            kb = k_ref[pl.ds(start, blk), :].astype(F32)
            km_ref[pl.ds(n, 1), :] = jnp.mean(kb, axis=0, keepdims=True)
            kt_ref[0:dh, pl.ds(start, blk)] = kb.T.astype(BF16)
            kt_ref[dh:dh + LANES, pl.ds(start, blk)] = (sub == n).astype(BF16)
            return carry

        lax.fori_loop(0, nb, fill, 0)

    q = q_ref[...]

    km = km_ref[...]
    km_hi = km.astype(BF16)
    km_lo = (km - km_hi.astype(F32)).astype(BF16)
    nt = (((1,), (1,)), ((), ()))
    gate = (lax.dot_general(q, km_hi, nt, preferred_element_type=F32)
            + lax.dot_general(q, km_lo, nt, preferred_element_type=F32))
    lane = lax.broadcasted_iota(jnp.int32, (blk, LANES), 1)
    gate = jnp.where(lane < i, gate, -jnp.inf)
    chosen = lane == i
    for _ in range(MOBA_TOPK):
        top = jnp.max(gate, axis=-1, keepdims=True)
        idx = jnp.min(jnp.where(gate == top, lane, LANES), axis=-1, keepdims=True)
        pick = lane == idx
        chosen = chosen | (pick & (lane < i))
        gate = jnp.where(pick, -jnp.inf, gate)
    penalty = jnp.where(chosen, 0.0, MASK_VALUE).astype(BF16)
    qa = jnp.concatenate([q, penalty], axis=-1)

    zero_m0 = (seq - LANES)

    def bias_tile(delta):
        top_m0 = pl.multiple_of(zero_m0 - blk * delta, LANES)
        bot_m0 = pl.multiple_of(zero_m0 - blk * delta - LANES, LANES)
        return jnp.concatenate([bias_ref[:, pl.ds(top_m0, blk)], bias_ref[:, pl.ds(bot_m0, blk)]], axis=0)

    def logits(j):
        start = pl.multiple_of(j * blk, blk)
        s = jnp.dot(qa, kt_ref[:, pl.ds(start, blk)], preferred_element_type=F32)
        return s + bias_tile(i - j), start

    s, start = logits(i)
    r = lax.broadcasted_iota(jnp.int32, (blk, blk), 0)
    c = lax.broadcasted_iota(jnp.int32, (blk, blk), 1)
    s = jnp.where(r >= c, s, MASK_VALUE)
    m0 = jnp.max(s, axis=-1, keepdims=True)
    p = jnp.exp(s - m0)
    m_ref[...] = jnp.broadcast_to(m0, m_ref.shape)
    l_ref[...] = jnp.broadcast_to(jnp.sum(p, axis=-1, keepdims=True), l_ref.shape)
    acc_ref[...] = jnp.dot(p.astype(BF16), v_ref[pl.ds(start, blk), :], preferred_element_type=F32)

    def step(j, carry):
        s, start = logits(j)
        m_prev = m_ref[...]
        m_new = jnp.maximum(m_prev, jnp.max(s, axis=-1, keepdims=True))
        alpha = jnp.exp(m_prev - m_new)
        p = jnp.exp(s - jnp.concatenate([m_new] * (blk // LANES), axis=-1))
        l_ref[...] = alpha * l_ref[...] + jnp.sum(p, axis=-1, keepdims=True)
        acc_ref[...] = alpha * acc_ref[...] + jnp.dot(
            p.astype(BF16), v_ref[pl.ds(start, blk), :], preferred_element_type=F32)
        m_ref[...] = m_new
        return carry

    lax.fori_loop(0, i, step, 0)
    o_ref[...] = (acc_ref[...] / l_ref[...]).astype(o_ref.dtype)


def _moba_attention(q, k, v, bias_win, b, s, heads):
    dh, blk = ATT_HEAD_DIM, MOBA_BLOCK
    assert dh == LANES and s % blk == 0 and s // blk <= LANES
    qspec = pl.BlockSpec((None, blk, dh), lambda bi, h, i: (bi, i, h))
    kvspec = pl.BlockSpec((None, s, dh), lambda bi, h, i: (bi, 0, h))
    return pl.pallas_call(
        functools.partial(_moba_kernel, seq=s),
        out_shape=jax.ShapeDtypeStruct((b, s, heads * dh), BF16),
        grid=(b, heads, s // blk),
        in_specs=[qspec, kvspec, kvspec,
                  pl.BlockSpec((None, LANES, s + LANES), lambda bi, h, i: (h, 0, 0))],
        out_specs=qspec,
        scratch_shapes=[
            pltpu.VMEM((dh + LANES, s), BF16),
            pltpu.VMEM((LANES, dh), F32),
            pltpu.VMEM((blk, LANES), F32),
            pltpu.VMEM((blk, LANES), F32),
            pltpu.VMEM((blk, dh), F32),
        ],
        compiler_params=_cparams(("arbitrary", "arbitrary", "arbitrary")),
        name="moba_attention",
    )(q, k, v, bias_win)


def _router_kernel(x_ref, g_ref, rw_ref, gates_ref, idx_ref):
    x = x_ref[...]
    xn = x * lax.rsqrt(jnp.mean(x * x, axis=-1, keepdims=True) + EPS) * g_ref[...]
    logits = jnp.dot(xn, rw_ref[...], preferred_element_type=F32, precision=lax.Precision.HIGHEST)
    lane = lax.broadcasted_iota(jnp.int32, logits.shape, 1)
    logits = jnp.where(lane < N_EXPERTS, logits, -jnp.inf)
    v1 = jnp.max(logits, axis=-1, keepdims=True)
    i1 = jnp.min(jnp.where(logits == v1, lane, LANES), axis=-1, keepdims=True)
    rest = jnp.where(lane == i1, -jnp.inf, logits)
    v2 = jnp.max(rest, axis=-1, keepdims=True)
    i2 = jnp.min(jnp.where(rest == v2, lane, LANES), axis=-1, keepdims=True)
    e2 = jnp.exp(v2 - v1)
    w1 = 1.0 / (1.0 + e2)
    w2 = e2 / (1.0 + e2)
    gates_ref[...] = jnp.where(lane == 0, w1, jnp.where(lane == 1, w2, 0.0))
    idx_ref[...] = jnp.where(lane == 0, i1, jnp.where(lane == 1, i2, 0))


def _router(h, norm_g, router_w):
    t, d = h.shape
    tm = min(256, t)
    rw = jnp.zeros((d, LANES), F32).at[:, :N_EXPERTS].set(router_w.astype(F32))
    row = pl.BlockSpec((tm, d), lambda i: (i, 0))
    out = pl.BlockSpec((tm, LANES), lambda i: (i, 0))
    return pl.pallas_call(
        _router_kernel,
        out_shape=[jax.ShapeDtypeStruct((t, LANES), F32), jax.ShapeDtypeStruct((t, LANES), jnp.int32)],
        grid=(t // tm,),
        in_specs=[row, pl.BlockSpec((1, d), lambda i: (0, 0)), pl.BlockSpec((d, LANES), lambda i: (0, 0))],
        out_specs=[out, out],
        compiler_params=_cparams(("arbitrary",)),
        name="moe_router",
    )(h, norm_g.reshape(1, d).astype(F32), rw)


def _row_copy(src_hbm, dst, sem, src_row, dst_row):
    return pltpu.make_async_copy(src_hbm.at[pl.ds(src_row, 1), :], dst.at[pl.ds(dst_row, 1), :], sem)


def _start_row_gather(idx_ref, src_hbm, dst, sem, rows):
    def start(r, carry):
        _row_copy(src_hbm, dst, sem, idx_ref[0, r], r).start()
        return carry

    lax.fori_loop(0, rows, start, 0)


def _wait_row_gather(src_hbm, dst, sem, rows):
    def wait(r, carry):
        _row_copy(src_hbm, dst, sem, 0, r).wait()
        return carry

    lax.fori_loop(0, rows, wait, 0)


def _gather_norm_kernel(tok_ref, h_hbm, g_ref, o_ref, buf, sem, *, rows):
    _start_row_gather(tok_ref, h_hbm, buf, sem, rows)
    _wait_row_gather(h_hbm, buf, sem, rows)
    x = buf[...]
    y = x * lax.rsqrt(jnp.mean(x * x, axis=-1, keepdims=True) + EPS)
    o_ref[...] = (y * g_ref[...]).astype(o_ref.dtype)


def _gather_norm(h, norm_g, tok_of):
    t, d = h.shape
    p = tok_of.shape[0]
    rows = min(GATHER_ROWS, p)
    assert p % rows == 0
    return pl.pallas_call(
        functools.partial(_gather_norm_kernel, rows=rows),
        out_shape=jax.ShapeDtypeStruct((p, d), BF16),
        grid=(p // rows,),
        in_specs=[
            pl.BlockSpec((None, 1, rows), lambda i: (i, 0, 0), memory_space=pltpu.SMEM),
            pl.BlockSpec(memory_space=pl.ANY),
            pl.BlockSpec((1, d), lambda i: (0, 0)),
        ],
        out_specs=pl.BlockSpec((rows, d), lambda i: (i, 0)),
        scratch_shapes=[pltpu.VMEM((rows, d), F32), pltpu.SemaphoreType.DMA],
        compiler_params=_cparams(("arbitrary",)),
        name="moe_gather_norm",
    )(tok_of.reshape(p // rows, 1, rows), h, norm_g.reshape(1, d).astype(F32))


def _moe_gu_kernel(te_ref, nu_ref, x_ref, wg_ref, wu_ref, o_ref):
    @pl.when(pl.program_id(0) < nu_ref[0])
    def _():
        x = x_ref[...]
        g = jnp.dot(x, wg_ref[...], preferred_element_type=F32)
        u = jnp.dot(x, wu_ref[...], preferred_element_type=F32)
        o_ref[...] = (g * _sigmoid(g) * u).astype(o_ref.dtype)

    @pl.when(pl.program_id(0) >= nu_ref[0])
    def _():
        o_ref[...] = jnp.zeros_like(o_ref)


def _moe_gu(xs, w_gu, tile_expert, n_used, *, tn):
    p, k = xs.shape
    f = w_gu.shape[2] // 2
    tm = min(MOE_TM, p)
    tn = min(tn, f)
    nj = f // tn

    def wmap(off):
        return lambda t, j, te, nu: (te[t], 0, jnp.where(t < nu[0], j, 0) + off)

    grid_spec = pltpu.PrefetchScalarGridSpec(
        num_scalar_prefetch=2,
        grid=(p // tm, nj),
        in_specs=[
            pl.BlockSpec((tm, k), lambda t, j, te, nu: (t, 0)),
            pl.BlockSpec((None, k, tn), wmap(0)),
            pl.BlockSpec((None, k, tn), wmap(nj)),
        ],
        out_specs=pl.BlockSpec((tm, tn), lambda t, j, te, nu: (t, j)),
    )
    return pl.pallas_call(
        _moe_gu_kernel,
        out_shape=jax.ShapeDtypeStruct((p, f), BF16),
        grid_spec=grid_spec,
        compiler_params=_cparams(("arbitrary", "arbitrary")),
        name="moe_gate_up",
    )(tile_expert, n_used, xs, w_gu, w_gu)


def _moe_down_kernel(te_ref, nu_ref, x_ref, w_ref, rw_ref, o_ref):
    @pl.when(pl.program_id(0) < nu_ref[0])
    def _():
        y = jnp.dot(x_ref[...], w_ref[...], preferred_element_type=F32)
        o_ref[...] = y * rw_ref[...]

    @pl.when(pl.program_id(0) >= nu_ref[0])
    def _():
        o_ref[...] = jnp.zeros_like(o_ref)


def _moe_down(ha, w_down, row_w, tile_expert, n_used, *, tn):
    p, k = ha.shape
    n = w_down.shape[2]
    tm = min(MOE_TM, p)
    tn = min(tn, n)
    grid_spec = pltpu.PrefetchScalarGridSpec(
        num_scalar_prefetch=2,
        grid=(p // tm, n // tn),
        in_specs=[
            pl.BlockSpec((tm, k), lambda t, j, te, nu: (t, 0)),
            pl.BlockSpec((None, k, tn), lambda t, j, te, nu: (te[t], 0, jnp.where(t < nu[0], j, 0))),
            pl.BlockSpec((tm, 1), lambda t, j, te, nu: (t, 0)),
        ],
        out_specs=pl.BlockSpec((tm, tn), lambda t, j, te, nu: (t, j)),
    )
    return pl.pallas_call(
        _moe_down_kernel,
        out_shape=jax.ShapeDtypeStruct((p, n), F32),
        grid_spec=grid_spec,
        compiler_params=_cparams(("arbitrary", "arbitrary")),
        name="moe_down",
    )(tile_expert, n_used, ha, w_down, row_w.reshape(p, 1))


def _combine_kernel(p1_ref, p2_ref, h_ref, ys_hbm, o_ref, buf1, buf2, sem1, sem2, *, rows):
    _start_row_gather(p1_ref, ys_hbm, buf1, sem1, rows)
    _start_row_gather(p2_ref, ys_hbm, buf2, sem2, rows)
    _wait_row_gather(ys_hbm, buf1, sem1, rows)
    _wait_row_gather(ys_hbm, buf2, sem2, rows)
    o_ref[...] = h_ref[...] + (buf1[...] + buf2[...])


def _moe_combine(h, ys, pos1, pos2):
    t, d = h.shape
    rows = min(COMBINE_ROWS, t)
    idx = pl.BlockSpec((None, 1, rows), lambda i: (i, 0, 0), memory_space=pltpu.SMEM)
    row = pl.BlockSpec((rows, d), lambda i: (i, 0))
    return pl.pallas_call(
        functools.partial(_combine_kernel, rows=rows),
        out_shape=jax.ShapeDtypeStruct((t, d), F32),
        grid=(t // rows,),
        in_specs=[idx, idx, row, pl.BlockSpec(memory_space=pl.ANY)],
        out_specs=row,
        scratch_shapes=[pltpu.VMEM((rows, d), F32), pltpu.VMEM((rows, d), F32),
                        pltpu.SemaphoreType.DMA, pltpu.SemaphoreType.DMA],
        compiler_params=_cparams(("arbitrary",)),
        name="moe_combine",
    )(pos1.reshape(t // rows, 1, rows), pos2.reshape(t // rows, 1, rows), h, ys)


def _moe_plan(idx, gates, tm):
    t = idx.shape[0]
    e_flat = idx[:, :2].reshape(-1)
    w_flat = gates[:, :2].reshape(-1)
    onehot = (e_flat[:, None] == jnp.arange(N_EXPERTS)[None, :]).astype(jnp.int32)
    csum = jnp.cumsum(onehot, axis=0)
    rank = jnp.sum((csum - onehot) * onehot, axis=1)
    counts = csum[-1]
    padded = ((counts + tm - 1) // tm) * tm
    ends = jnp.cumsum(padded)
    starts = ends - padded
    pos = jnp.sum(onehot * starts[None, :], axis=1) + rank
    p_rows = 2 * t + N_EXPERTS * tm
    tok_of = jnp.zeros((p_rows,), jnp.int32).at[pos].set(jnp.arange(2 * t, dtype=jnp.int32) // 2)
    row_w = jnp.zeros((p_rows,), F32).at[pos].set(w_flat)
    n_tiles = p_rows // tm
    n_used = (ends[-1] // tm).astype(jnp.int32)
    tile_start = jnp.arange(n_tiles, dtype=jnp.int32) * tm
    tile_expert = jnp.sum((tile_start[:, None] >= ends[None, :]).astype(jnp.int32), axis=1)
    last_expert = jnp.sum(((ends[-1] - 1) >= ends).astype(jnp.int32))
    tile_expert = jnp.where(tile_start < ends[-1], tile_expert, last_expert).astype(jnp.int32)
    pos2d = pos.reshape(t, 2).astype(jnp.int32)
    return tok_of, row_w, tile_expert, n_used.reshape(1), pos2d[:, 0], pos2d[:, 1]


def kernel(x, ret_norm_g, ret_w_in, ret_gn_g, ret_w_out, kv_norm_g, kv_w, k_norm_g, rel_bias, att_norm_g, att_w_q, q_norm_g, att_w_out, ffn_norm_g, ffn_w_gu, ffn_w_down, moe_norm_g, moe_router, moe_w_gu, moe_w_down):
    b, s, d = x.shape
    t = b * s
    ret_heads = d // RET_QK_DIM
    att_heads = d // ATT_HEAD_DIM
    att_dim = att_heads * ATT_HEAD_DIM
    h0 = x.reshape(t, d)

    (xn,) = _rmsnorm(h0, [ret_norm_g[0]], "ret_norm")
    proj = _matmul(xn, ret_w_in[0].astype(BF16), tm=1024, tn=1024, out_dtype=BF16, name="ret_in_proj")
    y = _retention(proj.reshape(b, s, -1), ret_gn_g[0], b, s, ret_heads).reshape(t, -1)
    h1 = _matmul(y, ret_w_out[0].astype(BF16), tm=1024, tn=512, out_dtype=F32, mode="res", res=h0,
                 x_buffers=1, name="ret_out_proj")

    (xn,) = _rmsnorm(h1, [ffn_norm_g[0]], "ffn_norm")
    a = _swiglu(xn, ffn_w_gu[0].astype(BF16), tm=1024, tn=256, name="ffn_gate_up")
    h2 = _matmul(a, ffn_w_down[0].astype(BF16), tm=1024, tn=256, out_dtype=F32, mode="res", res=h1,
                 x_buffers=1, name="ffn_down")

    xkv, xq = _rmsnorm(h2, [kv_norm_g, att_norm_g[0]], "kv_att_norm")
    kv_wb = kv_w.astype(BF16)
    k = _matmul(xkv, kv_wb[:, :att_dim], tm=1024, tn=1024, out_dtype=BF16, mode="headnorm", gain=k_norm_g,
                name="k_proj")
    v = _matmul(xkv, kv_wb[:, att_dim:], tm=1024, tn=1024, out_dtype=BF16, name="v_proj")
    q = _matmul(xq, att_w_q[0].astype(BF16), tm=1024, tn=1024, out_dtype=BF16, mode="headnorm",
                gain=q_norm_g[0], scale=ATT_HEAD_DIM ** -0.5, name="q_proj")
    bias_win = _bias_windows(rel_bias, s)
    o = _moba_attention(q.reshape(b, s, att_dim), k.reshape(b, s, att_dim), v.reshape(b, s, att_dim),
                        bias_win, b, s, att_heads).reshape(t, att_dim)
    h3 = _matmul(o, att_w_out[0].astype(BF16), tm=1024, tn=1024, out_dtype=F32, mode="res", res=h2,
                 name="att_out_proj")

    gates, idx = _router(h3, moe_norm_g[0], moe_router[0])
    tok_of, row_w, tile_expert, n_used, pos1, pos2 = _moe_plan(idx, gates, min(MOE_TM, 2 * t))
    xs = _gather_norm(h3, moe_norm_g[0], tok_of)
    ha = _moe_gu(xs, moe_w_gu[0].astype(BF16), tile_expert, n_used, tn=512)
    ys = _moe_down(ha, moe_w_down[0].astype(BF16), row_w, tile_expert, n_used, tn=1024)
    out = _moe_combine(h3, ys, pos1, pos2)
    return out.reshape(b, s, d)
```

```python
import functools
import math

import jax
import jax.numpy as jnp
from jax import lax
from jax.experimental import pallas as pl
from jax.experimental.pallas import tpu as pltpu

F32 = jnp.float32
BF16 = jnp.bfloat16

RET_QK_DIM = 256
RET_V_DIM = 512
ROPE_BASE = 10000.0
ATT_HEAD_DIM = 128
MOBA_BLOCK = 256
MOBA_TOPK = 3
REL_BUCKETS = 32
REL_MAX_DISTANCE = 4096
N_EXPERTS = 8
EPS = 1e-6

LANES = 128
V7X_VMEM_LIMIT_BYTES = 56 * 1024 * 1024
MASK_VALUE = -1e30

LOG2E = math.log2(math.e)

RET_CHUNK = 256
ATT_KEYS_PER_STEP = 1024
ATT_Q_PER_STEP = 1024
BIAS_ZERO = ATT_KEYS_PER_STEP - LANES
ONES_ROWS = 16
MOE_TM = 512
GATHER_ROWS = 256
COMBINE_ROWS = 128


def _cparams(sem):
    return pltpu.CompilerParams(dimension_semantics=sem, vmem_limit_bytes=V7X_VMEM_LIMIT_BYTES)


def _sigmoid(x):
    return 1.0 / (1.0 + jnp.exp(-x))


def _rmsnorm_kernel(x_ref, *refs):
    n = len(refs) // 2
    x = x_ref[...]
    y = x * lax.rsqrt(jnp.mean(x * x, axis=-1, keepdims=True) + EPS)
    for g_ref, o_ref in zip(refs[:n], refs[n:]):
        o_ref[...] = (y * g_ref[...]).astype(o_ref.dtype)


def _rmsnorm(x, gains, name):
    t, d = x.shape
    tm = min(256, t)
    n = len(gains)
    row = pl.BlockSpec((tm, d), lambda i: (i, 0))
    gspec = pl.BlockSpec((1, d), lambda i: (0, 0))
    outs = pl.pallas_call(
        _rmsnorm_kernel,
        out_shape=[jax.ShapeDtypeStruct((t, d), BF16)] * n,
        grid=(t // tm,),
        in_specs=[row] + [gspec] * n,
        out_specs=[row] * n,
        compiler_params=_cparams(("arbitrary",)),
        name=name,
    )(x, *[g.reshape(1, d).astype(F32) for g in gains])
    return outs


def _mm_kernel(x_ref, w_ref, *refs, mode, scale):
    o_ref = refs[-1]
    acc = jnp.dot(x_ref[...], w_ref[...], preferred_element_type=F32)
    if mode == "plain":
        o_ref[...] = acc.astype(o_ref.dtype)
    elif mode == "res":
        o_ref[...] = refs[0][...] + acc
    elif mode == "headnorm":
        g = refs[0][...] * scale
        for c in range(acc.shape[1] // LANES):
            a = acc[:, c * LANES:(c + 1) * LANES]
            y = a * lax.rsqrt(jnp.mean(a * a, axis=-1, keepdims=True) + EPS)
            o_ref[:, c * LANES:(c + 1) * LANES] = (y * g).astype(o_ref.dtype)
    else:
        raise ValueError(mode)


def _matmul(x, w, *, tm, tn, out_dtype, name, mode="plain", res=None, gain=None, scale=1.0, x_buffers=2):
    m, k = x.shape
    n = w.shape[1]
    tm, tn = min(tm, m), min(tn, n)
    assert m % tm == 0 and n % tn == 0, (m, n, tm, tn)
    xspec = pl.BlockSpec((tm, k), lambda i, j: (i, 0), pipeline_mode=pl.Buffered(x_buffers))
    in_specs = [xspec, pl.BlockSpec((k, tn), lambda i, j: (0, j))]
    args = [x, w]
    if mode == "res":
        in_specs.append(pl.BlockSpec((tm, tn), lambda i, j: (i, j)))
        args.append(res)
    elif mode == "headnorm":
        in_specs.append(pl.BlockSpec((1, LANES), lambda i, j: (0, 0)))
        args.append(gain.reshape(1, LANES).astype(F32))
    return pl.pallas_call(
        functools.partial(_mm_kernel, mode=mode, scale=scale),
        out_shape=jax.ShapeDtypeStruct((m, n), out_dtype),
        grid=(m // tm, n // tn),
        in_specs=in_specs,
        out_specs=pl.BlockSpec((tm, tn), lambda i, j: (i, j)),
        compiler_params=_cparams(("arbitrary", "arbitrary")),
        name=name,
    )(*args)


def _swiglu_kernel(x_ref, wg_ref, wu_ref, o_ref):
    x = x_ref[...]
    g = jnp.dot(x, wg_ref[...], preferred_element_type=F32)
    u = jnp.dot(x, wu_ref[...], preferred_element_type=F32)
    o_ref[...] = (g * _sigmoid(g) * u).astype(o_ref.dtype)


def _swiglu(x, w_gu, *, tm, tn, name):
    m, k = x.shape
    f = w_gu.shape[1] // 2
    tm, tn = min(tm, m), min(tn, f)
    assert m % tm == 0 and f % tn == 0
    nj = f // tn
    return pl.pallas_call(
        _swiglu_kernel,
        out_shape=jax.ShapeDtypeStruct((m, f), BF16),
        grid=(m // tm, nj),
        in_specs=[
            pl.BlockSpec((tm, k), lambda i, j: (i, 0)),
            pl.BlockSpec((k, tn), lambda i, j: (0, j)),
            pl.BlockSpec((k, tn), lambda i, j: (0, j + nj)),
        ],
        out_specs=pl.BlockSpec((tm, tn), lambda i, j: (i, j)),
        compiler_params=_cparams(("arbitrary", "arbitrary")),
        name=name,
    )(x, w_gu, w_gu)


def _retention_kernel(lg_ref, q_ref, k_ref, v_ref, g_ref, cos_ref, sin_ref, gn_ref, o_ref, state_ref, *, chunk):
    h = pl.program_id(1)
    c = pl.program_id(2)

    @pl.when(c == 0)
    def _():
        state_ref[...] = jnp.zeros_like(state_ref)

    lg = lg_ref[h]
    cos = cos_ref[...]
    sin = sin_ref[...]
    half = RET_QK_DIM // 2

    def rot(x):
        x1, x2 = x[:, :half], x[:, half:]
        return jnp.concatenate([x1 * cos - x2 * sin, x1 * sin + x2 * cos], axis=-1)

    q = rot(q_ref[...].astype(F32))
    k = rot(k_ref[...].astype(F32)) * (RET_QK_DIM ** -0.5)
    v = v_ref[...]

    row = lax.broadcasted_iota(jnp.int32, (chunk, 1), 0).astype(F32)
    col = lax.broadcasted_iota(jnp.int32, (1, chunk), 1).astype(F32)
    diff = row - col
    decay_intra = jnp.where(diff >= 0, jnp.exp(lg * jnp.maximum(diff, 0.0)), 0.0)
    decay_q = jnp.exp(lg * (row + 1.0))
    decay_k = jnp.exp(lg * (chunk - 1.0 - row))
    decay_chunk = jnp.exp(jnp.full((1, 1), chunk, F32) * lg)

    qb = q.astype(BF16)
    scores = lax.dot_general(qb, k.astype(BF16), (((1,), (1,)), ((), ())), preferred_element_type=F32)
    scores = scores * decay_intra
    inner = jnp.dot(scores.astype(BF16), v, preferred_element_type=F32)
    state = state_ref[...]
    cross = jnp.dot(qb, state.astype(BF16), preferred_element_type=F32) * decay_q
    kd = (k * decay_k).astype(BF16)
    state_ref[...] = state * decay_chunk + lax.dot_general(
        kd, v, (((0,), (0,)), ((), ())), preferred_element_type=F32)

    o = inner + cross
    mu = jnp.mean(o, axis=-1, keepdims=True)
    oc = o - mu
    var = jnp.mean(oc * oc, axis=-1, keepdims=True)
    on = oc * lax.rsqrt(var + EPS) * gn_ref[...]
    gate = g_ref[...].astype(F32)
    o_ref[...] = (gate * _sigmoid(gate) * on).astype(o_ref.dtype)


def _retention(proj, gn_g, b, s, heads):
    dk, dv = RET_QK_DIM, RET_V_DIM
    chunk = min(RET_CHUNK, s)
    assert s % chunk == 0
    pos = jnp.arange(s, dtype=F32)
    inv = 1.0 / (ROPE_BASE ** jnp.linspace(0.0, 1.0, dk // 2, dtype=F32))
    ang = pos[:, None] * inv[None, :]
    cos, sin = jnp.cos(ang), jnp.sin(ang)
    log_gamma = jnp.log1p(-jnp.exp2(-5.0 - jnp.arange(heads, dtype=F32)))
    k_off = heads
    v_off = (2 * heads * dk) // dv
    g_off = v_off + heads
    grid_spec = pltpu.PrefetchScalarGridSpec(
        num_scalar_prefetch=1,
        grid=(b, heads, s // chunk),
        in_specs=[
            pl.BlockSpec((None, chunk, dk), lambda bi, h, c, lg: (bi, c, h)),
            pl.BlockSpec((None, chunk, dk), lambda bi, h, c, lg: (bi, c, k_off + h)),
            pl.BlockSpec((None, chunk, dv), lambda bi, h, c, lg: (bi, c, v_off + h)),
            pl.BlockSpec((None, chunk, dv), lambda bi, h, c, lg: (bi, c, g_off + h)),
            pl.BlockSpec((chunk, dk // 2), lambda bi, h, c, lg: (c, 0)),
            pl.BlockSpec((chunk, dk // 2), lambda bi, h, c, lg: (c, 0)),
            pl.BlockSpec((1, dv), lambda bi, h, c, lg: (0, h)),
        ],
        out_specs=pl.BlockSpec((None, chunk, dv), lambda bi, h, c, lg: (bi, c, h)),
        scratch_shapes=[pltpu.VMEM((dk, dv), F32)],
    )
    return pl.pallas_call(
        functools.partial(_retention_kernel, chunk=chunk),
        out_shape=jax.ShapeDtypeStruct((b, s, heads * dv), BF16),
        grid_spec=grid_spec,
        compiler_params=_cparams(("arbitrary", "arbitrary", "arbitrary")),
        name="retention",
    )(log_gamma, proj, proj, proj, proj, cos, sin, gn_g.reshape(1, heads * dv).astype(F32))


def _rel_bucket(dist):
    n = jnp.maximum(dist, 0)
    max_exact = REL_BUCKETS // 2
    nf = jnp.maximum(n, max_exact).astype(F32)
    large = max_exact + (jnp.log(nf / max_exact) / math.log(REL_MAX_DISTANCE / max_exact)
                         * (REL_BUCKETS - max_exact)).astype(jnp.int32)
    large = jnp.minimum(large, REL_BUCKETS - 1)
    return jnp.where(n < max_exact, n, large)


def _bias_windows(rel_bias, s):
    u = jnp.arange(s + BIAS_ZERO + LANES - 1)
    dist = u - (BIAS_ZERO + LANES - 1)
    tab = jnp.where(dist[None, :] >= 0, LOG2E * rel_bias.astype(F32).T[:, _rel_bucket(dist)], MASK_VALUE)
    width = s + BIAS_ZERO
    return jnp.stack([tab[:, LANES - 1 - r: LANES - 1 - r + width] for r in range(LANES)], axis=1)


def _moba_kernel(q_ref, k_ref, v_ref, bias_ref, o_ref, ka_ref, vt_ref, km_ref, s_ref, *, seq):
    blk, dh, gk = MOBA_BLOCK, ATT_HEAD_DIM, ATT_KEYS_PER_STEP
    nb = seq // blk
    nq = q_ref.shape[0]
    q0 = pl.program_id(2) * nq

    @pl.when(q0 == 0)
    def _():
        km_ref[...] = jnp.zeros_like(km_ref)
        lane = lax.broadcasted_iota(jnp.int32, (blk, LANES), 1)

        def fill(n, carry):
            rows = pl.ds(pl.multiple_of(n * blk, blk), blk)
            kb = k_ref[rows, :]
            km_ref[pl.ds(n, 1), :] = jnp.mean(kb.astype(F32), axis=0, keepdims=True)
            ka_ref[rows, 0:dh] = kb
            ka_ref[rows, dh:dh + LANES] = (lane == n).astype(BF16)
            vt_ref[0:dh, rows] = v_ref[rows, :].astype(F32).T.astype(BF16)
            vt_ref[dh:dh + ONES_ROWS, rows] = jnp.ones((ONES_ROWS, blk), BF16)
            return carry

        lax.fori_loop(0, nb, fill, 0)

    qt = q_ref[...].astype(F32).T.astype(BF16)

    km = km_ref[...]
    km_hi = km.astype(BF16)
    km_lo = (km - km_hi.astype(F32)).astype(BF16)
    gate = (jnp.dot(km_hi, qt, preferred_element_type=F32)
            + jnp.dot(km_lo, qt, preferred_element_type=F32))
    bid = lax.broadcasted_iota(jnp.int32, (LANES, nq), 0)
    own = (q0 + lax.broadcasted_iota(jnp.int32, (LANES, nq), 1)) // blk
    past = bid < own
    gate = jnp.where(past, gate, -jnp.inf)
    chosen = bid == own
    for _ in range(MOBA_TOPK):
        top = jnp.max(gate, axis=0, keepdims=True)
        idx = jnp.min(jnp.where(gate == top, bid, LANES), axis=0, keepdims=True)
        pick = bid == idx
        chosen = chosen | (pick & past)
        gate = jnp.where(pick, -jnp.inf, gate)
    penalty = jnp.where(chosen, 0.0, MASK_VALUE).astype(BF16)
    qa = jnp.concatenate([qt, penalty], axis=0)

    halves = 2
    hk = gk // halves

    def score(g):
        k0 = g * gk
        m0 = BIAS_ZERO + q0 - k0
        top = None
        for t in range(halves):
            keys = pl.ds(pl.multiple_of(k0 + t * hk, hk), hk)
            bias = jnp.concatenate(
                [bias_ref[:, pl.ds(pl.multiple_of(m0 - t * hk - LANES * a, LANES), nq)]
                 for a in range(hk // LANES)], axis=0)
            st = jnp.dot(ka_ref[keys, :], qa, preferred_element_type=F32) + bias
            s_ref[t * hk:(t + 1) * hk, :] = st
            mt = jnp.max(st, axis=0, keepdims=True)
            top = mt if top is None else jnp.maximum(top, mt)
        return top

    def absorb(g, top, m_prev, acc):
        m_new = jnp.maximum(m_prev, top)
        acc = jnp.exp2(m_prev - m_new) * acc
        for t in range(halves):
            keys = pl.ds(pl.multiple_of(g * gk + t * hk, hk), hk)
            p = jnp.exp2(s_ref[t * hk:(t + 1) * hk, :] - m_new)
            acc = acc + jnp.dot(vt_ref[:, keys], p.astype(BF16), preferred_element_type=F32)
        return m_new, acc

    def step(g, carry):
        top, m_prev, acc = carry
        m_new, acc = absorb(g, top, m_prev, acc)
        return score(g + 1), m_new, acc

    last = q0 // gk
    init = (score(0), jnp.full((1, nq), MASK_VALUE, F32), jnp.zeros((dh + ONES_ROWS, nq), F32))
    top, m_prev, acc = lax.fori_loop(0, last, step, init)
    _, acc = absorb(last, top, m_prev, acc)
    o_ref[...] = (acc[0:dh, :] / acc[dh:dh + 1, :]).T.astype(o_ref.dtype)


def _moba_attention(q, k, v, bias_win, b, s, heads):
    dh, blk = ATT_HEAD_DIM, MOBA_BLOCK
    nq = min(ATT_Q_PER_STEP, s)
    assert dh == LANES and s % ATT_KEYS_PER_STEP == 0 and s // blk <= LANES
    assert ATT_KEYS_PER_STEP % nq == 0 and nq % blk == 0
    qspec = pl.BlockSpec((None, nq, dh), lambda bi, h, i: (bi, i, h))
    kvspec = pl.BlockSpec((None, s, dh), lambda bi, h, i: (bi, 0, h))
    return pl.pallas_call(
        functools.partial(_moba_kernel, seq=s),
        out_shape=jax.ShapeDtypeStruct((b, s, heads * dh), BF16),
        grid=(b, heads, s // nq),
        in_specs=[qspec, kvspec, kvspec,
                  pl.BlockSpec((None, LANES, s + BIAS_ZERO), lambda bi, h, i: (h, 0, 0))],
        out_specs=qspec,
        scratch_shapes=[
            pltpu.VMEM((s, dh + LANES), BF16),
            pltpu.VMEM((dh + ONES_ROWS, s), BF16),
            pltpu.VMEM((LANES, dh), F32),
            pltpu.VMEM((ATT_KEYS_PER_STEP, nq), F32),
        ],
        compiler_params=_cparams(("arbitrary", "arbitrary", "arbitrary")),
        name="moba_attention",
    )(q, k, v, bias_win)


def _router_kernel(x_ref, g_ref, rw_ref, gates_ref, idx_ref, cnt_ref):
    @pl.when(pl.program_id(0) == 0)
    def _():
        cnt_ref[...] = jnp.zeros_like(cnt_ref)

    x = x_ref[...]
    xn = x * lax.rsqrt(jnp.mean(x * x, axis=-1, keepdims=True) + EPS) * g_ref[...]
    logits = jnp.dot(xn, rw_ref[...], preferred_element_type=F32, precision=lax.Precision.HIGHEST)
    lane = lax.broadcasted_iota(jnp.int32, logits.shape, 1)
    logits = jnp.where(lane < N_EXPERTS, logits, -jnp.inf)
    v1 = jnp.max(logits, axis=-1, keepdims=True)
    i1 = jnp.min(jnp.where(logits == v1, lane, LANES), axis=-1, keepdims=True)
    rest = jnp.where(lane == i1, -jnp.inf, logits)
    v2 = jnp.max(rest, axis=-1, keepdims=True)
    i2 = jnp.min(jnp.where(rest == v2, lane, LANES), axis=-1, keepdims=True)
    e2 = jnp.exp(v2 - v1)
    w1 = 1.0 / (1.0 + e2)
    w2 = e2 / (1.0 + e2)
    gates_ref[...] = jnp.where(lane == 0, w1, jnp.where(lane == 1, w2, 0.0))

    tm = x.shape[0]
    picked = ((lane == i1) | (lane == i2)).astype(BF16)
    before = (lax.broadcasted_iota(jnp.int32, (tm, tm), 1)
              < lax.broadcasted_iota(jnp.int32, (tm, tm), 0)).astype(BF16)
    prior = jnp.dot(before, picked, preferred_element_type=F32) + cnt_ref[...]
    r1 = jnp.sum(jnp.where(lane == i1, prior, 0.0), axis=-1, keepdims=True).astype(jnp.int32)
    r2 = jnp.sum(jnp.where(lane == i2, prior, 0.0), axis=-1, keepdims=True).astype(jnp.int32)
    cnt_ref[...] += jnp.sum(picked.astype(F32), axis=0, keepdims=True)
    packed = jnp.where(lane == 3, r2, 0)
    for at, val in ((2, r1), (1, i2), (0, i1)):
        packed = jnp.where(lane == at, val, packed)
    idx_ref[...] = packed


def _router(h, norm_g, router_w):
    t, d = h.shape
    tm = min(256, t)
    rw = jnp.zeros((d, LANES), F32).at[:, :N_EXPERTS].set(router_w.astype(F32))
    row = pl.BlockSpec((tm, d), lambda i: (i, 0))
    out = pl.BlockSpec((tm, LANES), lambda i: (i, 0))
    return pl.pallas_call(
        _router_kernel,
        out_shape=[jax.ShapeDtypeStruct((t, LANES), F32), jax.ShapeDtypeStruct((t, LANES), jnp.int32),
                   jax.ShapeDtypeStruct((1, LANES), F32)],
        grid=(t // tm,),
        in_specs=[row, pl.BlockSpec((1, d), lambda i: (0, 0)), pl.BlockSpec((d, LANES), lambda i: (0, 0))],
        out_specs=[out, out, pl.BlockSpec((1, LANES), lambda i: (0, 0))],
        compiler_params=_cparams(("arbitrary",)),
        name="moe_router",
    )(h, norm_g.reshape(1, d).astype(F32), rw)


def _row_copy(src_hbm, dst, sem, src_row, dst_row):
    return pltpu.make_async_copy(src_hbm.at[pl.ds(src_row, 1), :], dst.at[pl.ds(dst_row, 1), :], sem)


def _start_row_gather(idx_ref, src_hbm, dst, sem, rows):
    def start(r, carry):
        _row_copy(src_hbm, dst, sem, idx_ref[0, r], r).start()
        return carry

    lax.fori_loop(0, rows, start, 0)


def _wait_row_gather(src_hbm, dst, sem, rows):
    def wait(r, carry):
        _row_copy(src_hbm, dst, sem, 0, r).wait()
        return carry

    lax.fori_loop(0, rows, wait, 0)


def _gather_norm_kernel(tok_ref, h_hbm, g_ref, o_ref, buf, sem, *, rows):
    _start_row_gather(tok_ref, h_hbm, buf, sem, rows)
    _wait_row_gather(h_hbm, buf, sem, rows)
    x = buf[...]
    y = x * lax.rsqrt(jnp.mean(x * x, axis=-1, keepdims=True) + EPS)
    o_ref[...] = (y * g_ref[...]).astype(o_ref.dtype)


def _gather_norm(h, norm_g, tok_of):
    t, d = h.shape
    p = tok_of.shape[0]
    rows = min(GATHER_ROWS, p)
    assert p % rows == 0
    return pl.pallas_call(
        functools.partial(_gather_norm_kernel, rows=rows),
        out_shape=jax.ShapeDtypeStruct((p, d), BF16),
        grid=(p // rows,),
        in_specs=[
            pl.BlockSpec((None, 1, rows), lambda i: (i, 0, 0), memory_space=pltpu.SMEM),
            pl.BlockSpec(memory_space=pl.ANY),
            pl.BlockSpec((1, d), lambda i: (0, 0)),
        ],
        out_specs=pl.BlockSpec((rows, d), lambda i: (i, 0)),
        scratch_shapes=[pltpu.VMEM((rows, d), F32), pltpu.SemaphoreType.DMA],
        compiler_params=_cparams(("arbitrary",)),
        name="moe_gather_norm",
    )(tok_of.reshape(p // rows, 1, rows), h, norm_g.reshape(1, d).astype(F32))


def _moe_gu_kernel(te_ref, nu_ref, x_ref, wg_ref, wu_ref, o_ref):
    @pl.when(pl.program_id(0) < nu_ref[0])
    def _():
        x = x_ref[...]
        g = jnp.dot(x, wg_ref[...], preferred_element_type=F32)
        u = jnp.dot(x, wu_ref[...], preferred_element_type=F32)
        o_ref[...] = (g * _sigmoid(g) * u).astype(o_ref.dtype)

    @pl.when(pl.program_id(0) >= nu_ref[0])
    def _():
        o_ref[...] = jnp.zeros_like(o_ref)


def _moe_gu(xs, w_gu, tile_expert, n_used, *, tn):
    p, k = xs.shape
    f = w_gu.shape[2] // 2
    tm = min(MOE_TM, p)
    tn = min(tn, f)
    nj = f // tn

    def wmap(off):
        return lambda t, j, te, nu: (te[t], 0, jnp.where(t < nu[0], j, 0) + off)

    grid_spec = pltpu.PrefetchScalarGridSpec(
        num_scalar_prefetch=2,
        grid=(p // tm, nj),
        in_specs=[
            pl.BlockSpec((tm, k), lambda t, j, te, nu: (t, 0)),
            pl.BlockSpec((None, k, tn), wmap(0)),
            pl.BlockSpec((None, k, tn), wmap(nj)),
        ],
        out_specs=pl.BlockSpec((tm, tn), lambda t, j, te, nu: (t, j)),
    )
    return pl.pallas_call(
        _moe_gu_kernel,
        out_shape=jax.ShapeDtypeStruct((p, f), BF16),
        grid_spec=grid_spec,
        compiler_params=_cparams(("arbitrary", "arbitrary")),
        name="moe_gate_up",
    )(tile_expert, n_used, xs, w_gu, w_gu)


def _moe_down_kernel(te_ref, nu_ref, x_ref, w_ref, o_ref):
    @pl.when(pl.program_id(0) < nu_ref[0])
    def _():
        o_ref[...] = jnp.dot(x_ref[...], w_ref[...], preferred_element_type=F32)

    @pl.when(pl.program_id(0) >= nu_ref[0])
    def _():
        o_ref[...] = jnp.zeros_like(o_ref)


def _moe_down(ha, w_down, tile_expert, n_used, *, tn):
    p, k = ha.shape
    n = w_down.shape[2]
    tm = min(MOE_TM, p)
    tn = min(tn, n)
    grid_spec = pltpu.PrefetchScalarGridSpec(
        num_scalar_prefetch=2,
        grid=(p // tm, n // tn),
        in_specs=[
            pl.BlockSpec((tm, k), lambda t, j, te, nu: (t, 0)),
            pl.BlockSpec((None, k, tn), lambda t, j, te, nu: (te[t], 0, jnp.where(t < nu[0], j, 0))),
        ],
        out_specs=pl.BlockSpec((tm, tn), lambda t, j, te, nu: (t, j)),
    )
    return pl.pallas_call(
        _moe_down_kernel,
        out_shape=jax.ShapeDtypeStruct((p, n), F32),
        grid_spec=grid_spec,
        compiler_params=_cparams(("arbitrary", "arbitrary")),
        name="moe_down",
    )(tile_expert, n_used, ha, w_down)


def _combine_kernel(p1_ref, p2_ref, h_ref, gates_ref, ys_hbm, o_ref, buf1, buf2, sem1, sem2, *, rows):
    _start_row_gather(p1_ref, ys_hbm, buf1, sem1, rows)
    _start_row_gather(p2_ref, ys_hbm, buf2, sem2, rows)
    _wait_row_gather(ys_hbm, buf1, sem1, rows)
    _wait_row_gather(ys_hbm, buf2, sem2, rows)
    gates = gates_ref[...]
    o_ref[...] = h_ref[...] + (gates[:, 0:1] * buf1[...] + gates[:, 1:2] * buf2[...])


def _moe_combine(h, gates, ys, pos1, pos2):
    t, d = h.shape
    rows = min(COMBINE_ROWS, t)
    idx = pl.BlockSpec((None, 1, rows), lambda i: (i, 0, 0), memory_space=pltpu.SMEM)
    row = pl.BlockSpec((rows, d), lambda i: (i, 0))
    return pl.pallas_call(
        functools.partial(_combine_kernel, rows=rows),
        out_shape=jax.ShapeDtypeStruct((t, d), F32),
        grid=(t // rows,),
        in_specs=[idx, idx, row, pl.BlockSpec((rows, LANES), lambda i: (i, 0)), pl.BlockSpec(memory_space=pl.ANY)],
        out_specs=row,
        scratch_shapes=[pltpu.VMEM((rows, d), F32), pltpu.VMEM((rows, d), F32),
                        pltpu.SemaphoreType.DMA, pltpu.SemaphoreType.DMA],
        compiler_params=_cparams(("arbitrary",)),
        name="moe_combine",
    )(pos1.reshape(t // rows, 1, rows), pos2.reshape(t // rows, 1, rows), h, gates, ys)


def _moe_plan(idx, counts, tm):
    t = idx.shape[0]
    counts = counts[0, :N_EXPERTS].astype(jnp.int32)
    padded = ((counts + tm - 1) // tm) * tm
    ends = jnp.cumsum(padded)
    starts = ends - padded
    experts = jnp.arange(N_EXPERTS, dtype=jnp.int32)[None, :]
    pos1 = jnp.sum(jnp.where(idx[:, 0:1] == experts, starts[None, :], 0), axis=1) + idx[:, 2]
    pos2 = jnp.sum(jnp.where(idx[:, 1:2] == experts, starts[None, :], 0), axis=1) + idx[:, 3]
    p_rows = 2 * t + N_EXPERTS * tm
    tok = jnp.arange(t, dtype=jnp.int32)
    tok_of = jnp.zeros((p_rows,), jnp.int32).at[jnp.concatenate([pos1, pos2])].set(jnp.concatenate([tok, tok]))
    n_tiles = p_rows // tm
    n_used = (ends[-1] // tm).astype(jnp.int32)
    tile_start = jnp.arange(n_tiles, dtype=jnp.int32) * tm
    tile_expert = jnp.sum((tile_start[:, None] >= ends[None, :]).astype(jnp.int32), axis=1)
    last_expert = jnp.sum(((ends[-1] - 1) >= ends).astype(jnp.int32))
    tile_expert = jnp.where(tile_start < ends[-1], tile_expert, last_expert).astype(jnp.int32)
    return tok_of, tile_expert, n_used.reshape(1), pos1.astype(jnp.int32), pos2.astype(jnp.int32)


def kernel(x, ret_norm_g, ret_w_in, ret_gn_g, ret_w_out, kv_norm_g, kv_w, k_norm_g, rel_bias, att_norm_g, att_w_q, q_norm_g, att_w_out, ffn_norm_g, ffn_w_gu, ffn_w_down, moe_norm_g, moe_router, moe_w_gu, moe_w_down):
    b, s, d = x.shape
    t = b * s
    ret_heads = d // RET_QK_DIM
    att_heads = d // ATT_HEAD_DIM
    att_dim = att_heads * ATT_HEAD_DIM
    h0 = x.reshape(t, d)

    (xn,) = _rmsnorm(h0, [ret_norm_g[0]], "ret_norm")
    proj = _matmul(xn, ret_w_in[0].astype(BF16), tm=1024, tn=1024, out_dtype=BF16, name="ret_in_proj")
    y = _retention(proj.reshape(b, s, -1), ret_gn_g[0], b, s, ret_heads).reshape(t, -1)
    h1 = _matmul(y, ret_w_out[0].astype(BF16), tm=1024, tn=512, out_dtype=F32, mode="res", res=h0,
                 x_buffers=1, name="ret_out_proj")

    (xn,) = _rmsnorm(h1, [ffn_norm_g[0]], "ffn_norm")
    a = _swiglu(xn, ffn_w_gu[0].astype(BF16), tm=1024, tn=256, name="ffn_gate_up")
    h2 = _matmul(a, ffn_w_down[0].astype(BF16), tm=1024, tn=256, out_dtype=F32, mode="res", res=h1,
                 x_buffers=1, name="ffn_down")

    xkv, xq = _rmsnorm(h2, [kv_norm_g, att_norm_g[0]], "kv_att_norm")
    kv_wb = kv_w.astype(BF16)
    k = _matmul(xkv, kv_wb[:, :att_dim], tm=1024, tn=1024, out_dtype=BF16, mode="headnorm", gain=k_norm_g,
                name="k_proj")
    v = _matmul(xkv, kv_wb[:, att_dim:], tm=1024, tn=1024, out_dtype=BF16, name="v_proj")
    q = _matmul(xq, att_w_q[0].astype(BF16), tm=1024, tn=1024, out_dtype=BF16, mode="headnorm",
                gain=q_norm_g[0], scale=LOG2E * ATT_HEAD_DIM ** -0.5, name="q_proj")
    bias_win = _bias_windows(rel_bias, s)
    o = _moba_attention(q.reshape(b, s, att_dim), k.reshape(b, s, att_dim), v.reshape(b, s, att_dim),
                        bias_win, b, s, att_heads).reshape(t, att_dim)
    h3 = _matmul(o, att_w_out[0].astype(BF16), tm=1024, tn=1024, out_dtype=F32, mode="res", res=h2,
                 name="att_out_proj")

    gates, idx, counts = _router(h3, moe_norm_g[0], moe_router[0])
    tok_of, tile_expert, n_used, pos1, pos2 = _moe_plan(idx, counts, min(MOE_TM, 2 * t))
    xs = _gather_norm(h3, moe_norm_g[0], tok_of)
    ha = _moe_gu(xs, moe_w_gu[0].astype(BF16), tile_expert, n_used, tn=512)
    ys = _moe_down(ha, moe_w_down[0].astype(BF16), tile_expert, n_used, tn=1024)
    out = _moe_combine(h3, gates, ys, pos1, pos2)
    return out.reshape(b, s, d)
```

```python
import functools
import math

import jax
import jax.numpy as jnp
from jax import lax
from jax.experimental import pallas as pl
from jax.experimental.pallas import tpu as pltpu

F32 = jnp.float32
BF16 = jnp.bfloat16

RET_QK_DIM = 256
RET_V_DIM = 512
ROPE_BASE = 10000.0
ATT_HEAD_DIM = 128
MOBA_BLOCK = 256
MOBA_TOPK = 3
REL_BUCKETS = 32
REL_MAX_DISTANCE = 4096
N_EXPERTS = 8
EPS = 1e-6

LANES = 128
V7X_VMEM_LIMIT_BYTES = 56 * 1024 * 1024
MASK_VALUE = -1e30

LOG2E = math.log2(math.e)

RET_CHUNK = 256
ATT_KEYS_PER_STEP = 1024
ATT_Q_PER_STEP = 1024
BIAS_ZERO = ATT_KEYS_PER_STEP - LANES
ONES_ROWS = 16
MOE_TM = 512
GATHER_ROWS = 256
COMBINE_ROWS = 128
GATHER_UNROLL = 8


def _cparams(sem):
    return pltpu.CompilerParams(dimension_semantics=sem, vmem_limit_bytes=V7X_VMEM_LIMIT_BYTES)


def _sigmoid(x):
    return 1.0 / (1.0 + jnp.exp(-x))


def _rmsnorm_kernel(x_ref, *refs):
    n = len(refs) // 2
    x = x_ref[...]
    y = x * lax.rsqrt(jnp.mean(x * x, axis=-1, keepdims=True) + EPS)
    for g_ref, o_ref in zip(refs[:n], refs[n:]):
        o_ref[...] = (y * g_ref[...]).astype(o_ref.dtype)


def _rmsnorm(x, gains, name):
    t, d = x.shape
    tm = min(256, t)
    n = len(gains)
    row = pl.BlockSpec((tm, d), lambda i: (i, 0))
    gspec = pl.BlockSpec((1, d), lambda i: (0, 0))
    outs = pl.pallas_call(
        _rmsnorm_kernel,
        out_shape=[jax.ShapeDtypeStruct((t, d), BF16)] * n,
        grid=(t // tm,),
        in_specs=[row] + [gspec] * n,
        out_specs=[row] * n,
        compiler_params=_cparams(("arbitrary",)),
        name=name,
    )(x, *[g.reshape(1, d).astype(F32) for g in gains])
    return outs


def _mm_kernel(x_ref, w_ref, *refs, mode, scale):
    o_ref = refs[-1]
    acc = jnp.dot(x_ref[...], w_ref[...], preferred_element_type=F32)
    if mode == "plain":
        o_ref[...] = acc.astype(o_ref.dtype)
    elif mode == "res":
        o_ref[...] = refs[0][...] + acc
    elif mode == "headnorm":
        g = refs[0][...] * scale
        for c in range(acc.shape[1] // LANES):
            a = acc[:, c * LANES:(c + 1) * LANES]
            y = a * lax.rsqrt(jnp.mean(a * a, axis=-1, keepdims=True) + EPS)
            o_ref[:, c * LANES:(c + 1) * LANES] = (y * g).astype(o_ref.dtype)
    else:
        raise ValueError(mode)


def _matmul(x, w, *, tm, tn, out_dtype, name, mode="plain", res=None, gain=None, scale=1.0, x_buffers=2):
    m, k = x.shape
    n = w.shape[1]
    tm, tn = min(tm, m), min(tn, n)
    assert m % tm == 0 and n % tn == 0, (m, n, tm, tn)
    xspec = pl.BlockSpec((tm, k), lambda i, j: (i, 0), pipeline_mode=pl.Buffered(x_buffers))
    in_specs = [xspec, pl.BlockSpec((k, tn), lambda i, j: (0, j))]
    args = [x, w]
    if mode == "res":
        in_specs.append(pl.BlockSpec((tm, tn), lambda i, j: (i, j)))
        args.append(res)
    elif mode == "headnorm":
        in_specs.append(pl.BlockSpec((1, LANES), lambda i, j: (0, 0)))
        args.append(gain.reshape(1, LANES).astype(F32))
    return pl.pallas_call(
        functools.partial(_mm_kernel, mode=mode, scale=scale),
        out_shape=jax.ShapeDtypeStruct((m, n), out_dtype),
        grid=(m // tm, n // tn),
        in_specs=in_specs,
        out_specs=pl.BlockSpec((tm, tn), lambda i, j: (i, j)),
        compiler_params=_cparams(("arbitrary", "arbitrary")),
        name=name,
    )(*args)


def _swiglu_kernel(x_ref, wg_ref, wu_ref, o_ref):
    x = x_ref[...]
    g = jnp.dot(x, wg_ref[...], preferred_element_type=F32)
    u = jnp.dot(x, wu_ref[...], preferred_element_type=F32)
    o_ref[...] = (g * _sigmoid(g) * u).astype(o_ref.dtype)


def _swiglu(x, w_gu, *, tm, tn, name):
    m, k = x.shape
    f = w_gu.shape[1] // 2
    tm, tn = min(tm, m), min(tn, f)
    assert m % tm == 0 and f % tn == 0
    nj = f // tn
    return pl.pallas_call(
        _swiglu_kernel,
        out_shape=jax.ShapeDtypeStruct((m, f), BF16),
        grid=(m // tm, nj),
        in_specs=[
            pl.BlockSpec((tm, k), lambda i, j: (i, 0)),
            pl.BlockSpec((k, tn), lambda i, j: (0, j)),
            pl.BlockSpec((k, tn), lambda i, j: (0, j + nj)),
        ],
        out_specs=pl.BlockSpec((tm, tn), lambda i, j: (i, j)),
        compiler_params=_cparams(("arbitrary", "arbitrary")),
        name=name,
    )(x, w_gu, w_gu)


def _retention_kernel(lg_ref, q_ref, k_ref, v_ref, g_ref, cos_ref, sin_ref, gn_ref, o_ref, state_ref, intra_ref,
                      dq_ref, dk_ref, *, chunk):
    h = pl.program_id(1)
    c = pl.program_id(2)
    lg = lg_ref[h]

    @pl.when(c == 0)
    def _():
        state_ref[...] = jnp.zeros_like(state_ref)
        row = lax.broadcasted_iota(jnp.int32, (chunk, 1), 0).astype(F32)
        col = lax.broadcasted_iota(jnp.int32, (1, chunk), 1).astype(F32)
        diff = row - col
        intra_ref[...] = jnp.where(diff >= 0, jnp.exp(lg * jnp.maximum(diff, 0.0)), 0.0)
        dq_ref[...] = jnp.broadcast_to(jnp.exp(lg * (row + 1.0)), dq_ref.shape)
        dk_ref[...] = jnp.broadcast_to(jnp.exp(lg * (chunk - 1.0 - row)), dk_ref.shape)

    cos = cos_ref[...]
    sin = sin_ref[...]
    half = RET_QK_DIM // 2

    def rot(x):
        x1, x2 = x[:, :half], x[:, half:]
        return jnp.concatenate([x1 * cos - x2 * sin, x1 * sin + x2 * cos], axis=-1)

    q = rot(q_ref[...].astype(F32))
    k = rot(k_ref[...].astype(F32)) * (RET_QK_DIM ** -0.5)
    v = v_ref[...]

    decay_chunk = jnp.exp(jnp.full((1, 1), chunk, F32) * lg)

    qb = q.astype(BF16)
    scores = lax.dot_general(qb, k.astype(BF16), (((1,), (1,)), ((), ())), preferred_element_type=F32)
    scores = scores * intra_ref[...]
    inner = jnp.dot(scores.astype(BF16), v, preferred_element_type=F32)
    state = state_ref[...]
    decay_q = jnp.concatenate([dq_ref[...]] * (RET_V_DIM // LANES), axis=-1)
    cross = jnp.dot(qb, state.astype(BF16), preferred_element_type=F32) * decay_q
    kd = (k * jnp.concatenate([dk_ref[...]] * (RET_QK_DIM // LANES), axis=-1)).astype(BF16)
    state_ref[...] = state * decay_chunk + lax.dot_general(
        kd, v, (((0,), (0,)), ((), ())), preferred_element_type=F32)

    o = inner + cross
    mu = jnp.mean(o, axis=-1, keepdims=True)
    oc = o - mu
    var = jnp.mean(oc * oc, axis=-1, keepdims=True)
    on = oc * lax.rsqrt(var + EPS) * gn_ref[...]
    gate = g_ref[...].astype(F32)
    o_ref[...] = (gate * _sigmoid(gate) * on).astype(o_ref.dtype)


def _retention(proj, gn_g, b, s, heads):
    dk, dv = RET_QK_DIM, RET_V_DIM
    chunk = min(RET_CHUNK, s)
    assert s % chunk == 0
    pos = jnp.arange(s, dtype=F32)
    inv = 1.0 / (ROPE_BASE ** jnp.linspace(0.0, 1.0, dk // 2, dtype=F32))
    ang = pos[:, None] * inv[None, :]
    cos, sin = jnp.cos(ang), jnp.sin(ang)
    log_gamma = jnp.log1p(-jnp.exp2(-5.0 - jnp.arange(heads, dtype=F32)))
    k_off = heads
    v_off = (2 * heads * dk) // dv
    g_off = v_off + heads
    grid_spec = pltpu.PrefetchScalarGridSpec(
        num_scalar_prefetch=1,
        grid=(b, heads, s // chunk),
        in_specs=[
            pl.BlockSpec((None, chunk, dk), lambda bi, h, c, lg: (bi, c, h)),
            pl.BlockSpec((None, chunk, dk), lambda bi, h, c, lg: (bi, c, k_off + h)),
            pl.BlockSpec((None, chunk, dv), lambda bi, h, c, lg: (bi, c, v_off + h)),
            pl.BlockSpec((None, chunk, dv), lambda bi, h, c, lg: (bi, c, g_off + h)),
            pl.BlockSpec((chunk, dk // 2), lambda bi, h, c, lg: (c, 0)),
            pl.BlockSpec((chunk, dk // 2), lambda bi, h, c, lg: (c, 0)),
            pl.BlockSpec((1, dv), lambda bi, h, c, lg: (0, h)),
        ],
        out_specs=pl.BlockSpec((None, chunk, dv), lambda bi, h, c, lg: (bi, c, h)),
        scratch_shapes=[
            pltpu.VMEM((dk, dv), F32),
            pltpu.VMEM((chunk, chunk), F32),
            pltpu.VMEM((chunk, LANES), F32),
            pltpu.VMEM((chunk, LANES), F32),
        ],
    )
    return pl.pallas_call(
        functools.partial(_retention_kernel, chunk=chunk),
        out_shape=jax.ShapeDtypeStruct((b, s, heads * dv), BF16),
        grid_spec=grid_spec,
        compiler_params=_cparams(("arbitrary", "arbitrary", "arbitrary")),
        name="retention",
    )(log_gamma, proj, proj, proj, proj, cos, sin, gn_g.reshape(1, heads * dv).astype(F32))


def _rel_bucket(dist):
    n = jnp.maximum(dist, 0)
    max_exact = REL_BUCKETS // 2
    nf = jnp.maximum(n, max_exact).astype(F32)
    large = max_exact + (jnp.log(nf / max_exact) / math.log(REL_MAX_DISTANCE / max_exact)
                         * (REL_BUCKETS - max_exact)).astype(jnp.int32)
    large = jnp.minimum(large, REL_BUCKETS - 1)
    return jnp.where(n < max_exact, n, large)


def _bias_windows(rel_bias, s):
    width = s + BIAS_ZERO
    heads = rel_bias.shape[1]
    u = jnp.arange(width + LANES)
    dist = u - (BIAS_ZERO + LANES - 1)
    tab = jnp.where(dist[None, :] >= 0, LOG2E * rel_bias.astype(F32).T[:, _rel_bucket(dist)], MASK_VALUE)
    return pl.pallas_call(
        _bias_window_kernel,
        out_shape=jax.ShapeDtypeStruct((heads, LANES, width), F32),
        grid=(heads,),
        in_specs=[pl.BlockSpec((None, 1, width + LANES), lambda h: (h, 0, 0))],
        out_specs=pl.BlockSpec((None, LANES, width), lambda h: (h, 0, 0)),
        compiler_params=_cparams(("arbitrary",)),
        name="bias_windows",
    )(tab.reshape(heads, 1, width + LANES))


def _bias_window_kernel(tab_ref, o_ref):
    w = tab_ref.shape[-1]
    x = jnp.broadcast_to(tab_ref[...], (LANES, w))
    shifted = pltpu.roll(x, w - (LANES - 1), 1, stride=1, stride_axis=0)
    o_ref[...] = shifted[:, :o_ref.shape[-1]]


def _moba_kernel(q_ref, k_ref, v_ref, bias_ref, o_ref, ka_ref, vt_ref, km_ref, s_ref, *, seq):
    blk, dh, gk = MOBA_BLOCK, ATT_HEAD_DIM, ATT_KEYS_PER_STEP
    nb = seq // blk
    nq = q_ref.shape[0]
    q0 = pl.program_id(2) * nq

    @pl.when(q0 == 0)
    def _():
        km_ref[...] = jnp.zeros_like(km_ref)
        lane = lax.broadcasted_iota(jnp.int32, (blk, LANES), 1)

        def fill(n, carry):
            rows = pl.ds(pl.multiple_of(n * blk, blk), blk)
            kb = k_ref[rows, :]
            km_ref[pl.ds(n, 1), :] = jnp.mean(kb.astype(F32), axis=0, keepdims=True)
            ka_ref[rows, 0:dh] = kb
            ka_ref[rows, dh:dh + LANES] = (lane == n).astype(BF16)
            vt_ref[0:dh, rows] = v_ref[rows, :].astype(F32).T.astype(BF16)
            vt_ref[dh:dh + ONES_ROWS, rows] = jnp.ones((ONES_ROWS, blk), BF16)
            return carry

        lax.fori_loop(0, nb, fill, 0)

    qt = q_ref[...].astype(F32).T.astype(BF16)

    km = km_ref[...]
    km_hi = km.astype(BF16)
    km_lo = (km - km_hi.astype(F32)).astype(BF16)
    gate = (jnp.dot(km_hi, qt, preferred_element_type=F32)
            + jnp.dot(km_lo, qt, preferred_element_type=F32))
    bid = lax.broadcasted_iota(jnp.int32, (LANES, nq), 0)
    own = (q0 + lax.broadcasted_iota(jnp.int32, (LANES, nq), 1)) // blk
    past = bid < own
    gate = jnp.where(past, gate, -jnp.inf)
    chosen = bid == own
    for _ in range(MOBA_TOPK):
        top = jnp.max(gate, axis=0, keepdims=True)
        idx = jnp.min(jnp.where(gate == top, bid, LANES), axis=0, keepdims=True)
        pick = bid == idx
        chosen = chosen | (pick & past)
        gate = jnp.where(pick, -jnp.inf, gate)
    penalty = jnp.where(chosen, 0.0, MASK_VALUE).astype(BF16)
    qa = jnp.concatenate([qt, penalty], axis=0)

    halves = 4
    hk = gk // halves

    def score(g):
        k0 = g * gk
        m0 = BIAS_ZERO + q0 - k0
        top = None
        for t in range(halves):
            keys = pl.ds(pl.multiple_of(k0 + t * hk, hk), hk)
            bias = jnp.concatenate(
                [bias_ref[:, pl.ds(pl.multiple_of(m0 - t * hk - LANES * a, LANES), nq)]
                 for a in range(hk // LANES)], axis=0)
            st = jnp.dot(ka_ref[keys, :], qa, preferred_element_type=F32) + bias
            s_ref[t * hk:(t + 1) * hk, :] = st
            mt = jnp.max(st, axis=0, keepdims=True)
            top = mt if top is None else jnp.maximum(top, mt)
        return top

    def absorb(g, top, m_prev, acc):
        m_new = jnp.maximum(m_prev, top)
        acc = jnp.exp2(m_prev - m_new) * acc
        for t in range(halves):
            keys = pl.ds(pl.multiple_of(g * gk + t * hk, hk), hk)
            p = jnp.exp2(s_ref[t * hk:(t + 1) * hk, :] - m_new)
            acc = acc + jnp.dot(vt_ref[:, keys], p.astype(BF16), preferred_element_type=F32)
        return m_new, acc

    def step(g, carry):
        top, m_prev, acc = carry
        m_new, acc = absorb(g, top, m_prev, acc)
        return score(g + 1), m_new, acc

    last = q0 // gk
    init = (score(0), jnp.full((1, nq), MASK_VALUE, F32), jnp.zeros((dh + ONES_ROWS, nq), F32))
    top, m_prev, acc = lax.fori_loop(0, last, step, init)
    _, acc = absorb(last, top, m_prev, acc)
    o_ref[...] = (acc[0:dh, :] / acc[dh:dh + 1, :]).T.astype(o_ref.dtype)


def _moba_attention(q, k, v, bias_win, b, s, heads):
    dh, blk = ATT_HEAD_DIM, MOBA_BLOCK
    nq = min(ATT_Q_PER_STEP, s)
    assert dh == LANES and s % ATT_KEYS_PER_STEP == 0 and s // blk <= LANES
    assert ATT_KEYS_PER_STEP % nq == 0 and nq % blk == 0
    qspec = pl.BlockSpec((None, nq, dh), lambda bi, h, i: (bi, i, h))
    kvspec = pl.BlockSpec((None, s, dh), lambda bi, h, i: (bi, 0, h))
    return pl.pallas_call(
        functools.partial(_moba_kernel, seq=s),
        out_shape=jax.ShapeDtypeStruct((b, s, heads * dh), BF16),
        grid=(b, heads, s // nq),
        in_specs=[qspec, kvspec, kvspec,
                  pl.BlockSpec((None, LANES, s + BIAS_ZERO), lambda bi, h, i: (h, 0, 0))],
        out_specs=qspec,
        scratch_shapes=[
            pltpu.VMEM((s, dh + LANES), BF16),
            pltpu.VMEM((dh + ONES_ROWS, s), BF16),
            pltpu.VMEM((LANES, dh), F32),
            pltpu.VMEM((ATT_KEYS_PER_STEP, nq), F32),
        ],
        compiler_params=_cparams(("arbitrary", "arbitrary", "arbitrary")),
        name="moba_attention",
    )(q, k, v, bias_win)


def _router_kernel(x_ref, g_ref, rw_ref, gates_ref, idx_ref, cnt_ref):
    @pl.when(pl.program_id(0) == 0)
    def _():
        cnt_ref[...] = jnp.zeros_like(cnt_ref)

    x = x_ref[...]
    xn = x * lax.rsqrt(jnp.mean(x * x, axis=-1, keepdims=True) + EPS) * g_ref[...]
    logits = jnp.dot(xn, rw_ref[...], preferred_element_type=F32, precision=lax.Precision.HIGHEST)
    lane = lax.broadcasted_iota(jnp.int32, logits.shape, 1)
    logits = jnp.where(lane < N_EXPERTS, logits, -jnp.inf)
    v1 = jnp.max(logits, axis=-1, keepdims=True)
    i1 = jnp.min(jnp.where(logits == v1, lane, LANES), axis=-1, keepdims=True)
    rest = jnp.where(lane == i1, -jnp.inf, logits)
    v2 = jnp.max(rest, axis=-1, keepdims=True)
    i2 = jnp.min(jnp.where(rest == v2, lane, LANES), axis=-1, keepdims=True)
    e2 = jnp.exp(v2 - v1)
    w1 = 1.0 / (1.0 + e2)
    w2 = e2 / (1.0 + e2)
    gates_ref[...] = jnp.where(lane == 0, w1, jnp.where(lane == 1, w2, 0.0))

    tm = x.shape[0]
    picked = ((lane == i1) | (lane == i2)).astype(BF16)
    before = (lax.broadcasted_iota(jnp.int32, (tm, tm), 1)
              < lax.broadcasted_iota(jnp.int32, (tm, tm), 0)).astype(BF16)
    prior = jnp.dot(before, picked, preferred_element_type=F32) + cnt_ref[...]
    r1 = jnp.sum(jnp.where(lane == i1, prior, 0.0), axis=-1, keepdims=True).astype(jnp.int32)
    r2 = jnp.sum(jnp.where(lane == i2, prior, 0.0), axis=-1, keepdims=True).astype(jnp.int32)
    cnt_ref[...] += jnp.sum(picked.astype(F32), axis=0, keepdims=True)
    packed = jnp.where(lane == 3, r2, 0)
    for at, val in ((2, r1), (1, i2), (0, i1)):
        packed = jnp.where(lane == at, val, packed)
    idx_ref[...] = packed


def _router(h, norm_g, router_w):
    t, d = h.shape
    tm = min(256, t)
    rw = jnp.zeros((d, LANES), F32).at[:, :N_EXPERTS].set(router_w.astype(F32))
    row = pl.BlockSpec((tm, d), lambda i: (i, 0))
    out = pl.BlockSpec((tm, LANES), lambda i: (i, 0))
    return pl.pallas_call(
        _router_kernel,
        out_shape=[jax.ShapeDtypeStruct((t, LANES), F32), jax.ShapeDtypeStruct((t, LANES), jnp.int32),
                   jax.ShapeDtypeStruct((1, LANES), F32)],
        grid=(t // tm,),
        in_specs=[row, pl.BlockSpec((1, d), lambda i: (0, 0)), pl.BlockSpec((d, LANES), lambda i: (0, 0))],
        out_specs=[out, out, pl.BlockSpec((1, LANES), lambda i: (0, 0))],
        compiler_params=_cparams(("arbitrary",)),
        name="moe_router",
    )(h, norm_g.reshape(1, d).astype(F32), rw)


def _row_copy(src_hbm, dst, sem, src_row, dst_row):
    return pltpu.make_async_copy(src_hbm.at[pl.ds(src_row, 1), :], dst.at[pl.ds(dst_row, 1), :], sem)


def _start_row_gather(idx_ref, src_hbm, dst, sem, rows):
    def start(r, carry):
        _row_copy(src_hbm, dst, sem, idx_ref[0, r], r).start()
        return carry

    lax.fori_loop(0, rows, start, 0, unroll=GATHER_UNROLL)


def _wait_row_gather(src_hbm, dst, sem, rows):
    def wait(r, carry):
        _row_copy(src_hbm, dst, sem, 0, r).wait()
        return carry

    lax.fori_loop(0, rows, wait, 0, unroll=GATHER_UNROLL)


def _prefetched_row_gather(idx_refs, nxt_refs, src_hbm, bufs, sems, rows):
    i = pl.program_id(0)
    slot = i % 2

    @pl.when(i == 0)
    def _():
        for idx_ref, buf, sem in zip(idx_refs, bufs, sems):
            _start_row_gather(idx_ref, src_hbm, buf.at[0], sem.at[0], rows)

    @pl.when(i + 1 < pl.num_programs(0))
    def _():
        for nxt_ref, buf, sem in zip(nxt_refs, bufs, sems):
            _start_row_gather(nxt_ref, src_hbm, buf.at[1 - slot], sem.at[1 - slot], rows)

    for buf, sem in zip(bufs, sems):
        _wait_row_gather(src_hbm, buf.at[slot], sem.at[slot], rows)
    return slot


def _idx_specs(n_steps, rows):
    cur = pl.BlockSpec((None, 1, rows), lambda i: (i, 0, 0), memory_space=pltpu.SMEM)
    nxt = pl.BlockSpec((None, 1, rows), lambda i: (jnp.minimum(i + 1, n_steps - 1), 0, 0),
                       memory_space=pltpu.SMEM)
    return cur, nxt


def _gather_norm_kernel(tok_ref, nxt_ref, h_hbm, g_ref, o_ref, buf, sem, *, rows):
    slot = _prefetched_row_gather([tok_ref], [nxt_ref], h_hbm, [buf], [sem], rows)
    x = buf[slot]
    y = x * lax.rsqrt(jnp.mean(x * x, axis=-1, keepdims=True) + EPS)
    o_ref[...] = (y * g_ref[...]).astype(o_ref.dtype)


def _gather_norm(h, norm_g, tok_of):
    t, d = h.shape
    p = tok_of.shape[0]
    rows = min(GATHER_ROWS, p)
    assert p % rows == 0
    n_steps = p // rows
    cur, nxt = _idx_specs(n_steps, rows)
    tok3 = tok_of.reshape(n_steps, 1, rows)
    return pl.pallas_call(
        functools.partial(_gather_norm_kernel, rows=rows),
        out_shape=jax.ShapeDtypeStruct((p, d), BF16),
        grid=(n_steps,),
        in_specs=[cur, nxt, pl.BlockSpec(memory_space=pl.ANY), pl.BlockSpec((1, d), lambda i: (0, 0))],
        out_specs=pl.BlockSpec((rows, d), lambda i: (i, 0)),
        scratch_shapes=[pltpu.VMEM((2, rows, d), F32), pltpu.SemaphoreType.DMA((2,))],
        compiler_params=_cparams(("arbitrary",)),
        name="moe_gather_norm",
    )(tok3, tok3, h, norm_g.reshape(1, d).astype(F32))


def _moe_gu_kernel(te_ref, nu_ref, x_ref, wg_ref, wu_ref, o_ref):
    @pl.when(pl.program_id(0) < nu_ref[0])
    def _():
        x = x_ref[...]
        g = jnp.dot(x, wg_ref[...], preferred_element_type=F32)
        u = jnp.dot(x, wu_ref[...], preferred_element_type=F32)
        o_ref[...] = (g * _sigmoid(g) * u).astype(o_ref.dtype)

    @pl.when(pl.program_id(0) >= nu_ref[0])
    def _():
        o_ref[...] = jnp.zeros_like(o_ref)


def _moe_gu(xs, w_gu, tile_expert, n_used, *, tn):
    p, k = xs.shape
    f = w_gu.shape[2] // 2
    tm = min(MOE_TM, p)
    tn = min(tn, f)
    nj = f // tn

    def wmap(off):
        return lambda t, j, te, nu: (te[t], 0, jnp.where(t < nu[0], j, 0) + off)

    grid_spec = pltpu.PrefetchScalarGridSpec(
        num_scalar_prefetch=2,
        grid=(p // tm, nj),
        in_specs=[
            pl.BlockSpec((tm, k), lambda t, j, te, nu: (t, 0)),
            pl.BlockSpec((None, k, tn), wmap(0)),
            pl.BlockSpec((None, k, tn), wmap(nj)),
        ],
        out_specs=pl.BlockSpec((tm, tn), lambda t, j, te, nu: (t, j)),
    )
    return pl.pallas_call(
        _moe_gu_kernel,
        out_shape=jax.ShapeDtypeStruct((p, f), BF16),
        grid_spec=grid_spec,
        compiler_params=_cparams(("arbitrary", "arbitrary")),
        name="moe_gate_up",
    )(tile_expert, n_used, xs, w_gu, w_gu)


def _moe_down_kernel(te_ref, nu_ref, x_ref, w_ref, o_ref):
    @pl.when(pl.program_id(0) < nu_ref[0])
    def _():
        o_ref[...] = jnp.dot(x_ref[...], w_ref[...], preferred_element_type=F32)

    @pl.when(pl.program_id(0) >= nu_ref[0])
    def _():
        o_ref[...] = jnp.zeros_like(o_ref)


def _moe_down(ha, w_down, tile_expert, n_used, *, tn):
    p, k = ha.shape
    n = w_down.shape[2]
    tm = min(MOE_TM, p)
    tn = min(tn, n)
    grid_spec = pltpu.PrefetchScalarGridSpec(
        num_scalar_prefetch=2,
        grid=(p // tm, n // tn),
        in_specs=[
            pl.BlockSpec((tm, k), lambda t, j, te, nu: (t, 0)),
            pl.BlockSpec((None, k, tn), lambda t, j, te, nu: (te[t], 0, jnp.where(t < nu[0], j, 0))),
        ],
        out_specs=pl.BlockSpec((tm, tn), lambda t, j, te, nu: (t, j)),
    )
    return pl.pallas_call(
        _moe_down_kernel,
        out_shape=jax.ShapeDtypeStruct((p, n), F32),
        grid_spec=grid_spec,
        compiler_params=_cparams(("arbitrary", "arbitrary")),
        name="moe_down",
    )(tile_expert, n_used, ha, w_down)


def _combine_kernel(p1_ref, n1_ref, p2_ref, n2_ref, h_ref, gates_ref, ys_hbm, o_ref, buf1, buf2, sem1, sem2, *,
                    rows):
    slot = _prefetched_row_gather([p1_ref, p2_ref], [n1_ref, n2_ref], ys_hbm, [buf1, buf2], [sem1, sem2], rows)
    gates = gates_ref[...]
    o_ref[...] = h_ref[...] + (gates[:, 0:1] * buf1[slot] + gates[:, 1:2] * buf2[slot])


def _moe_combine(h, gates, ys, pos1, pos2):
    t, d = h.shape
    rows = min(COMBINE_ROWS, t)
    n_steps = t // rows
    cur, nxt = _idx_specs(n_steps, rows)
    row = pl.BlockSpec((rows, d), lambda i: (i, 0))
    p1 = pos1.reshape(n_steps, 1, rows)
    p2 = pos2.reshape(n_steps, 1, rows)
    return pl.pallas_call(
        functools.partial(_combine_kernel, rows=rows),
        out_shape=jax.ShapeDtypeStruct((t, d), F32),
        grid=(n_steps,),
        in_specs=[cur, nxt, cur, nxt, row, pl.BlockSpec((rows, LANES), lambda i: (i, 0)),
                  pl.BlockSpec(memory_space=pl.ANY)],
        out_specs=row,
        scratch_shapes=[pltpu.VMEM((2, rows, d), F32), pltpu.VMEM((2, rows, d), F32),
                        pltpu.SemaphoreType.DMA((2,)), pltpu.SemaphoreType.DMA((2,))],
        compiler_params=_cparams(("arbitrary",)),
        name="moe_combine",
    )(p1, p1, p2, p2, h, gates, ys)


def _moe_plan(idx, counts, tm):
    t = idx.shape[0]
    counts = counts[0, :N_EXPERTS].astype(jnp.int32)
    padded = ((counts + tm - 1) // tm) * tm
    ends = jnp.cumsum(padded)
    starts = ends - padded
    experts = jnp.arange(N_EXPERTS, dtype=jnp.int32)[None, :]
    pos1 = jnp.sum(jnp.where(idx[:, 0:1] == experts, starts[None, :], 0), axis=1) + idx[:, 2]
    pos2 = jnp.sum(jnp.where(idx[:, 1:2] == experts, starts[None, :], 0), axis=1) + idx[:, 3]
    p_rows = 2 * t + N_EXPERTS * tm
    tok = jnp.arange(t, dtype=jnp.int32)
    tok_of = jnp.zeros((p_rows,), jnp.int32).at[jnp.concatenate([pos1, pos2])].set(jnp.concatenate([tok, tok]))
    n_tiles = p_rows // tm
    n_used = (ends[-1] // tm).astype(jnp.int32)
    tile_start = jnp.arange(n_tiles, dtype=jnp.int32) * tm
    tile_expert = jnp.sum((tile_start[:, None] >= ends[None, :]).astype(jnp.int32), axis=1)
    last_expert = jnp.sum(((ends[-1] - 1) >= ends).astype(jnp.int32))
    tile_expert = jnp.where(tile_start < ends[-1], tile_expert, last_expert).astype(jnp.int32)
    return tok_of, tile_expert, n_used.reshape(1), pos1.astype(jnp.int32), pos2.astype(jnp.int32)


def kernel(x, ret_norm_g, ret_w_in, ret_gn_g, ret_w_out, kv_norm_g, kv_w, k_norm_g, rel_bias, att_norm_g, att_w_q, q_norm_g, att_w_out, ffn_norm_g, ffn_w_gu, ffn_w_down, moe_norm_g, moe_router, moe_w_gu, moe_w_down):
    b, s, d = x.shape
    t = b * s
    ret_heads = d // RET_QK_DIM
    att_heads = d // ATT_HEAD_DIM
    att_dim = att_heads * ATT_HEAD_DIM
    h0 = x.reshape(t, d)

    (xn,) = _rmsnorm(h0, [ret_norm_g[0]], "ret_norm")
    proj = _matmul(xn, ret_w_in[0].astype(BF16), tm=1024, tn=1024, out_dtype=BF16, name="ret_in_proj")
    y = _retention(proj.reshape(b, s, -1), ret_gn_g[0], b, s, ret_heads).reshape(t, -1)
    h1 = _matmul(y, ret_w_out[0].astype(BF16), tm=1024, tn=512, out_dtype=F32, mode="res", res=h0,
                 x_buffers=1, name="ret_out_proj")

    (xn,) = _rmsnorm(h1, [ffn_norm_g[0]], "ffn_norm")
    a = _swiglu(xn, ffn_w_gu[0].astype(BF16), tm=1024, tn=256, name="ffn_gate_up")
    h2 = _matmul(a, ffn_w_down[0].astype(BF16), tm=1024, tn=256, out_dtype=F32, mode="res", res=h1,
                 x_buffers=1, name="ffn_down")

    xkv, xq = _rmsnorm(h2, [kv_norm_g, att_norm_g[0]], "kv_att_norm")
    kv_wb = kv_w.astype(BF16)
    k = _matmul(xkv, kv_wb[:, :att_dim], tm=1024, tn=1024, out_dtype=BF16, mode="headnorm", gain=k_norm_g,
                name="k_proj")
    v = _matmul(xkv, kv_wb[:, att_dim:], tm=1024, tn=1024, out_dtype=BF16, name="v_proj")
    q = _matmul(xq, att_w_q[0].astype(BF16), tm=1024, tn=1024, out_dtype=BF16, mode="headnorm",
                gain=q_norm_g[0], scale=LOG2E * ATT_HEAD_DIM ** -0.5, name="q_proj")
    bias_win = _bias_windows(rel_bias, s)
    o = _moba_attention(q.reshape(b, s, att_dim), k.reshape(b, s, att_dim), v.reshape(b, s, att_dim),
                        bias_win, b, s, att_heads).reshape(t, att_dim)
    h3 = _matmul(o, att_w_out[0].astype(BF16), tm=1024, tn=1024, out_dtype=F32, mode="res", res=h2,
                 name="att_out_proj")

    gates, idx, counts = _router(h3, moe_norm_g[0], moe_router[0])
    tok_of, tile_expert, n_used, pos1, pos2 = _moe_plan(idx, counts, min(MOE_TM, 2 * t))
    xs = _gather_norm(h3, moe_norm_g[0], tok_of)
    ha = _moe_gu(xs, moe_w_gu[0].astype(BF16), tile_expert, n_used, tn=1024)
    ys = _moe_down(ha, moe_w_down[0].astype(BF16), tile_expert, n_used, tn=1024)
    out = _moe_combine(h3, gates, ys, pos1, pos2)
    return out.reshape(b, s, d)
```

```python
import functools
import math

import jax
import jax.numpy as jnp
from jax import lax
from jax.experimental import pallas as pl
from jax.experimental.pallas import tpu as pltpu

F32 = jnp.float32
BF16 = jnp.bfloat16

RET_QK_DIM = 256
RET_V_DIM = 512
ROPE_BASE = 10000.0
ATT_HEAD_DIM = 128
MOBA_BLOCK = 256
MOBA_TOPK = 3
REL_BUCKETS = 32
REL_MAX_DISTANCE = 4096
N_EXPERTS = 8
EPS = 1e-6

LANES = 128
BF16_SUBLANES = 16
V7X_VMEM_LIMIT_BYTES = 56 * 1024 * 1024
MASK_VALUE = -1e30

LOG2E = math.log2(math.e)

RET_CHUNK = 256
ATT_KEYS_PER_STEP = 1024
ATT_Q_PER_STEP = 1024
BIAS_ZERO = ATT_KEYS_PER_STEP - LANES
ONES_ROWS = 16
MOE_TM = 512
GATHER_ROWS = 256
COMBINE_ROWS = 128
GATHER_UNROLL = 8


def _cparams(sem):
    return pltpu.CompilerParams(dimension_semantics=sem, vmem_limit_bytes=V7X_VMEM_LIMIT_BYTES)


def _sigmoid(x):
    return 1.0 / (1.0 + jnp.exp(-x))


def _rmsnorm_kernel(x_ref, *refs):
    n = len(refs) // 2
    x = x_ref[...]
    y = x * lax.rsqrt(jnp.mean(x * x, axis=-1, keepdims=True) + EPS)
    for g_ref, o_ref in zip(refs[:n], refs[n:]):
        o_ref[...] = (y * g_ref[...]).astype(o_ref.dtype)


def _rmsnorm(x, gains, name):
    t, d = x.shape
    tm = min(256, t)
    n = len(gains)
    row = pl.BlockSpec((tm, d), lambda i: (i, 0))
    gspec = pl.BlockSpec((1, d), lambda i: (0, 0))
    outs = pl.pallas_call(
        _rmsnorm_kernel,
        out_shape=[jax.ShapeDtypeStruct((t, d), BF16)] * n,
        grid=(t // tm,),
        in_specs=[row] + [gspec] * n,
        out_specs=[row] * n,
        compiler_params=_cparams(("arbitrary",)),
        name=name,
    )(x, *[g.reshape(1, d).astype(F32) for g in gains])
    return outs


def _cast_riders(riders, grid):
    steps = grid[0] * grid[1]
    specs, shapes = [], []
    for a in riders:
        rows, cols = a.shape
        nblk = next(n for n in range(min(steps, rows // BF16_SUBLANES), 0, -1)
                    if rows % n == 0 and (rows // n) % BF16_SUBLANES == 0)
        specs.append(pl.BlockSpec((rows // nblk, cols),
                                  lambda i, j, nblk=nblk: (jnp.minimum(i * grid[1] + j, nblk - 1), 0)))
        shapes.append(jax.ShapeDtypeStruct((rows, cols), BF16))
    return specs, shapes


def _run_riders(ins, outs):
    for src, dst in zip(ins, outs):
        dst[...] = src[...].astype(dst.dtype)


def _mm_kernel(x_ref, w_ref, *refs, mode, scale, n_riders):
    n_extra = 0 if mode == "plain" else 1
    extra = refs[:n_extra]
    o_ref = refs[n_extra + n_riders]
    _run_riders(refs[n_extra:n_extra + n_riders], refs[n_extra + n_riders + 1:])
    acc = jnp.dot(x_ref[...], w_ref[...], preferred_element_type=F32)
    if mode == "plain":
        o_ref[...] = acc.astype(o_ref.dtype)
    elif mode == "res":
        o_ref[...] = extra[0][...] + acc
    elif mode == "headnorm":
        g = extra[0][...] * scale
        for c in range(acc.shape[1] // LANES):
            a = acc[:, c * LANES:(c + 1) * LANES]
            y = a * lax.rsqrt(jnp.mean(a * a, axis=-1, keepdims=True) + EPS)
            o_ref[:, c * LANES:(c + 1) * LANES] = (y * g).astype(o_ref.dtype)
    else:
        raise ValueError(mode)


def _matmul(x, w, *, tm, tn, out_dtype, name, mode="plain", res=None, gain=None, scale=1.0, x_buffers=2,
            n_out=None, col0=0, riders=()):
    m, k = x.shape
    n = w.shape[1] if n_out is None else n_out
    tm, tn = min(tm, m), min(tn, n)
    assert m % tm == 0 and n % tn == 0 and col0 % tn == 0, (m, n, tm, tn, col0)
    grid = (m // tm, n // tn)
    jb = col0 // tn
    xspec = pl.BlockSpec((tm, k), lambda i, j: (i, 0), pipeline_mode=pl.Buffered(x_buffers))
    in_specs = [xspec, pl.BlockSpec((k, tn), lambda i, j: (0, j + jb))]
    args = [x, w]
    if mode == "res":
        in_specs.append(pl.BlockSpec((tm, tn), lambda i, j: (i, j)))
        args.append(res)
    elif mode == "headnorm":
        in_specs.append(pl.BlockSpec((1, LANES), lambda i, j: (0, 0)))
        args.append(gain.reshape(1, LANES).astype(F32))
    rider_specs, rider_shapes = _cast_riders(riders, grid)
    outs = pl.pallas_call(
        functools.partial(_mm_kernel, mode=mode, scale=scale, n_riders=len(riders)),
        out_shape=[jax.ShapeDtypeStruct((m, n), out_dtype)] + rider_shapes,
        grid=grid,
        in_specs=in_specs + rider_specs,
        out_specs=[pl.BlockSpec((tm, tn), lambda i, j: (i, j))] + rider_specs,
        compiler_params=_cparams(("arbitrary", "arbitrary")),
        name=name,
    )(*args, *riders)
    return outs[0], outs[1:]


def _swiglu_kernel(x_ref, wg_ref, wu_ref, *refs, n_riders):
    o_ref = refs[n_riders]
    _run_riders(refs[:n_riders], refs[n_riders + 1:])
    x = x_ref[...]
    g = jnp.dot(x, wg_ref[...], preferred_element_type=F32)
    u = jnp.dot(x, wu_ref[...], preferred_element_type=F32)
    o_ref[...] = (g * _sigmoid(g) * u).astype(o_ref.dtype)


def _swiglu(x, w_gu, *, tm, tn, name, riders=()):
    m, k = x.shape
    f = w_gu.shape[1] // 2
    tm, tn = min(tm, m), min(tn, f)
    assert m % tm == 0 and f % tn == 0
    nj = f // tn
    grid = (m // tm, nj)
    rider_specs, rider_shapes = _cast_riders(riders, grid)
    outs = pl.pallas_call(
        functools.partial(_swiglu_kernel, n_riders=len(riders)),
        out_shape=[jax.ShapeDtypeStruct((m, f), BF16)] + rider_shapes,
        grid=grid,
        in_specs=[
            pl.BlockSpec((tm, k), lambda i, j: (i, 0)),
            pl.BlockSpec((k, tn), lambda i, j: (0, j)),
            pl.BlockSpec((k, tn), lambda i, j: (0, j + nj)),
        ] + rider_specs,
        out_specs=[pl.BlockSpec((tm, tn), lambda i, j: (i, j))] + rider_specs,
        compiler_params=_cparams(("arbitrary", "arbitrary")),
        name=name,
    )(x, w_gu, w_gu, *riders)
    return outs[0], outs[1:]


def _retention_kernel(lg_ref, q_ref, k_ref, v_ref, g_ref, cos_ref, sin_ref, gn_ref, o_ref, state_ref, intra_ref,
                      dq_ref, dk_ref, *, chunk):
    h = pl.program_id(1)
    c = pl.program_id(2)
    lg = lg_ref[h]

    @pl.when(c == 0)
    def _():
        state_ref[...] = jnp.zeros_like(state_ref)
        row = lax.broadcasted_iota(jnp.int32, (chunk, 1), 0).astype(F32)
        col = lax.broadcasted_iota(jnp.int32, (1, chunk), 1).astype(F32)
        diff = row - col
        intra_ref[...] = jnp.where(diff >= 0, jnp.exp(lg * jnp.maximum(diff, 0.0)), 0.0)
        dq_ref[...] = jnp.broadcast_to(jnp.exp(lg * (row + 1.0)), dq_ref.shape)
        dk_ref[...] = jnp.broadcast_to(jnp.exp(lg * (chunk - 1.0 - row)), dk_ref.shape)

    cos = cos_ref[...]
    sin = sin_ref[...]
    half = RET_QK_DIM // 2

    def rot(x):
        x1, x2 = x[:, :half], x[:, half:]
        return jnp.concatenate([x1 * cos - x2 * sin, x1 * sin + x2 * cos], axis=-1)

    q = rot(q_ref[...].astype(F32))
    k = rot(k_ref[...].astype(F32)) * (RET_QK_DIM ** -0.5)
    v = v_ref[...]

    decay_chunk = jnp.exp(jnp.full((1, 1), chunk, F32) * lg)

    qb = q.astype(BF16)
    scores = lax.dot_general(qb, k.astype(BF16), (((1,), (1,)), ((), ())), preferred_element_type=F32)
    scores = scores * intra_ref[...]
    inner = jnp.dot(scores.astype(BF16), v, preferred_element_type=F32)
    state = state_ref[...]
    decay_q = jnp.concatenate([dq_ref[...]] * (RET_V_DIM // LANES), axis=-1)
    cross = jnp.dot(qb, state.astype(BF16), preferred_element_type=F32) * decay_q
    kd = (k * jnp.concatenate([dk_ref[...]] * (RET_QK_DIM // LANES), axis=-1)).astype(BF16)
    state_ref[...] = state * decay_chunk + lax.dot_general(
        kd, v, (((0,), (0,)), ((), ())), preferred_element_type=F32)

    o = inner + cross
    mu = jnp.mean(o, axis=-1, keepdims=True)
    oc = o - mu
    var = jnp.mean(oc * oc, axis=-1, keepdims=True)
    on = oc * lax.rsqrt(var + EPS) * gn_ref[...]
    gate = g_ref[...].astype(F32)
    o_ref[...] = (gate * _sigmoid(gate) * on).astype(o_ref.dtype)


def _retention(proj, gn_g, b, s, heads):
    dk, dv = RET_QK_DIM, RET_V_DIM
    chunk = min(RET_CHUNK, s)
    assert s % chunk == 0
    pos = jnp.arange(s, dtype=F32)
    inv = 1.0 / (ROPE_BASE ** jnp.linspace(0.0, 1.0, dk // 2, dtype=F32))
    ang = pos[:, None] * inv[None, :]
    cos, sin = jnp.cos(ang), jnp.sin(ang)
    log_gamma = jnp.log1p(-jnp.exp2(-5.0 - jnp.arange(heads, dtype=F32)))
    k_off = heads
    v_off = (2 * heads * dk) // dv
    g_off = v_off + heads
    grid_spec = pltpu.PrefetchScalarGridSpec(
        num_scalar_prefetch=1,
        grid=(b, heads, s // chunk),
        in_specs=[
            pl.BlockSpec((None, chunk, dk), lambda bi, h, c, lg: (bi, c, h)),
            pl.BlockSpec((None, chunk, dk), lambda bi, h, c, lg: (bi, c, k_off + h)),
            pl.BlockSpec((None, chunk, dv), lambda bi, h, c, lg: (bi, c, v_off + h)),
            pl.BlockSpec((None, chunk, dv), lambda bi, h, c, lg: (bi, c, g_off + h)),
            pl.BlockSpec((chunk, dk // 2), lambda bi, h, c, lg: (c, 0)),
            pl.BlockSpec((chunk, dk // 2), lambda bi, h, c, lg: (c, 0)),
            pl.BlockSpec((1, dv), lambda bi, h, c, lg: (0, h)),
        ],
        out_specs=pl.BlockSpec((None, chunk, dv), lambda bi, h, c, lg: (bi, c, h)),
        scratch_shapes=[
            pltpu.VMEM((dk, dv), F32),
            pltpu.VMEM((chunk, chunk), F32),
            pltpu.VMEM((chunk, LANES), F32),
            pltpu.VMEM((chunk, LANES), F32),
        ],
    )
    return pl.pallas_call(
        functools.partial(_retention_kernel, chunk=chunk),
        out_shape=jax.ShapeDtypeStruct((b, s, heads * dv), BF16),
        grid_spec=grid_spec,
        compiler_params=_cparams(("arbitrary", "arbitrary", "arbitrary")),
        name="retention",
    )(log_gamma, proj, proj, proj, proj, cos, sin, gn_g.reshape(1, heads * dv).astype(F32))


def _rel_bucket(dist):
    n = jnp.maximum(dist, 0)
    max_exact = REL_BUCKETS // 2
    nf = jnp.maximum(n, max_exact).astype(F32)
    large = max_exact + (jnp.log(nf / max_exact) / math.log(REL_MAX_DISTANCE / max_exact)
                         * (REL_BUCKETS - max_exact)).astype(jnp.int32)
    large = jnp.minimum(large, REL_BUCKETS - 1)
    return jnp.where(n < max_exact, n, large)


def _bias_windows(rel_bias, s):
    width = s + BIAS_ZERO
    heads = rel_bias.shape[1]
    u = jnp.arange(width + LANES)
    dist = u - (BIAS_ZERO + LANES - 1)
    tab = jnp.where(dist[None, :] >= 0, LOG2E * rel_bias.astype(F32).T[:, _rel_bucket(dist)], MASK_VALUE)
    return pl.pallas_call(
        _bias_window_kernel,
        out_shape=jax.ShapeDtypeStruct((heads, LANES, width), F32),
        grid=(heads,),
        in_specs=[pl.BlockSpec((None, 1, width + LANES), lambda h: (h, 0, 0))],
        out_specs=pl.BlockSpec((None, LANES, width), lambda h: (h, 0, 0)),
        compiler_params=_cparams(("arbitrary",)),
        name="bias_windows",
    )(tab.reshape(heads, 1, width + LANES))


def _bias_window_kernel(tab_ref, o_ref):
    w = tab_ref.shape[-1]
    x = jnp.broadcast_to(tab_ref[...], (LANES, w))
    shifted = pltpu.roll(x, w - (LANES - 1), 1, stride=1, stride_axis=0)
    o_ref[...] = shifted[:, :o_ref.shape[-1]]


def _moba_kernel(q_ref, k_ref, v_ref, bias_ref, o_ref, ka_ref, vt_ref, km_ref, s_ref, *, seq):
    blk, dh, gk = MOBA_BLOCK, ATT_HEAD_DIM, ATT_KEYS_PER_STEP
    nb = seq // blk
    nq = q_ref.shape[0]
    q0 = pl.program_id(2) * nq

    @pl.when(q0 == 0)
    def _():
        km_ref[...] = jnp.zeros_like(km_ref)
        lane = lax.broadcasted_iota(jnp.int32, (blk, LANES), 1)

        def fill(n, carry):
            rows = pl.ds(pl.multiple_of(n * blk, blk), blk)
            kb = k_ref[rows, :]
            km_ref[pl.ds(n, 1), :] = jnp.mean(kb.astype(F32), axis=0, keepdims=True)
            ka_ref[rows, 0:dh] = kb
            ka_ref[rows, dh:dh + LANES] = (lane == n).astype(BF16)
            vt_ref[0:dh, rows] = v_ref[rows, :].astype(F32).T.astype(BF16)
            vt_ref[dh:dh + ONES_ROWS, rows] = jnp.ones((ONES_ROWS, blk), BF16)
            return carry

        lax.fori_loop(0, nb, fill, 0)

    qt = q_ref[...].astype(F32).T.astype(BF16)

    km = km_ref[...]
    km_hi = km.astype(BF16)
    km_lo = (km - km_hi.astype(F32)).astype(BF16)
    gate = (jnp.dot(km_hi, qt, preferred_element_type=F32)
            + jnp.dot(km_lo, qt, preferred_element_type=F32))
    bid = lax.broadcasted_iota(jnp.int32, (LANES, nq), 0)
    own = (q0 + lax.broadcasted_iota(jnp.int32, (LANES, nq), 1)) // blk
    past = bid < own
    gate = jnp.where(past, gate, -jnp.inf)
    chosen = bid == own
    for _ in range(MOBA_TOPK):
        top = jnp.max(gate, axis=0, keepdims=True)
        idx = jnp.min(jnp.where(gate == top, bid, LANES), axis=0, keepdims=True)
        pick = bid == idx
        chosen = chosen | (pick & past)
        gate = jnp.where(pick, -jnp.inf, gate)
    penalty = jnp.where(chosen, 0.0, MASK_VALUE).astype(BF16)
    qa = jnp.concatenate([qt, penalty], axis=0)

    halves = 4
    hk = gk // halves

    def score(g):
        k0 = g * gk
        m0 = BIAS_ZERO + q0 - k0
        top = None
        for t in range(halves):
            keys = pl.ds(pl.multiple_of(k0 + t * hk, hk), hk)
            bias = jnp.concatenate(
                [bias_ref[:, pl.ds(pl.multiple_of(m0 - t * hk - LANES * a, LANES), nq)]
                 for a in range(hk // LANES)], axis=0)
            st = jnp.dot(ka_ref[keys, :], qa, preferred_element_type=F32) + bias
            s_ref[t * hk:(t + 1) * hk, :] = st
            mt = jnp.max(st, axis=0, keepdims=True)
            top = mt if top is None else jnp.maximum(top, mt)
        return top

    def absorb(g, top, m_prev, acc):
        m_new = jnp.maximum(m_prev, top)
        acc = jnp.exp2(m_prev - m_new) * acc
        for t in range(halves):
            keys = pl.ds(pl.multiple_of(g * gk + t * hk, hk), hk)
            p = jnp.exp2(s_ref[t * hk:(t + 1) * hk, :] - m_new)
            acc = acc + jnp.dot(vt_ref[:, keys], p.astype(BF16), preferred_element_type=F32)
        return m_new, acc

    def step(g, carry):
        top, m_prev, acc = carry
        m_new, acc = absorb(g, top, m_prev, acc)
        return score(g + 1), m_new, acc

    last = q0 // gk
    init = (score(0), jnp.full((1, nq), MASK_VALUE, F32), jnp.zeros((dh + ONES_ROWS, nq), F32))
    top, m_prev, acc = lax.fori_loop(0, last, step, init)
    _, acc = absorb(last, top, m_prev, acc)
    o_ref[...] = (acc[0:dh, :] / acc[dh:dh + 1, :]).T.astype(o_ref.dtype)


def _moba_attention(q, k, v, bias_win, b, s, heads):
    dh, blk = ATT_HEAD_DIM, MOBA_BLOCK
    nq = min(ATT_Q_PER_STEP, s)
    assert dh == LANES and s % ATT_KEYS_PER_STEP == 0 and s // blk <= LANES
    assert ATT_KEYS_PER_STEP % nq == 0 and nq % blk == 0
    qspec = pl.BlockSpec((None, nq, dh), lambda bi, h, i: (bi, i, h))
    kvspec = pl.BlockSpec((None, s, dh), lambda bi, h, i: (bi, 0, h))
    return pl.pallas_call(
        functools.partial(_moba_kernel, seq=s),
        out_shape=jax.ShapeDtypeStruct((b, s, heads * dh), BF16),
        grid=(b, heads, s // nq),
        in_specs=[qspec, kvspec, kvspec,
                  pl.BlockSpec((None, LANES, s + BIAS_ZERO), lambda bi, h, i: (h, 0, 0))],
        out_specs=qspec,
        scratch_shapes=[
            pltpu.VMEM((s, dh + LANES), BF16),
            pltpu.VMEM((dh + ONES_ROWS, s), BF16),
            pltpu.VMEM((LANES, dh), F32),
            pltpu.VMEM((ATT_KEYS_PER_STEP, nq), F32),
        ],
        compiler_params=_cparams(("arbitrary", "arbitrary", "arbitrary")),
        name="moba_attention",
    )(q, k, v, bias_win)


def _router_kernel(x_ref, g_ref, rw_ref, gates_ref, idx_ref, cnt_ref):
    @pl.when(pl.program_id(0) == 0)
    def _():
        cnt_ref[...] = jnp.zeros_like(cnt_ref)

    x = x_ref[...]
    xn = x * lax.rsqrt(jnp.mean(x * x, axis=-1, keepdims=True) + EPS) * g_ref[...]
    logits = jnp.dot(xn, rw_ref[...], preferred_element_type=F32, precision=lax.Precision.HIGHEST)
    lane = lax.broadcasted_iota(jnp.int32, logits.shape, 1)
    logits = jnp.where(lane < N_EXPERTS, logits, -jnp.inf)
    v1 = jnp.max(logits, axis=-1, keepdims=True)
    i1 = jnp.min(jnp.where(logits == v1, lane, LANES), axis=-1, keepdims=True)
    rest = jnp.where(lane == i1, -jnp.inf, logits)
    v2 = jnp.max(rest, axis=-1, keepdims=True)
    i2 = jnp.min(jnp.where(rest == v2, lane, LANES), axis=-1, keepdims=True)
    e2 = jnp.exp(v2 - v1)
    w1 = 1.0 / (1.0 + e2)
    w2 = e2 / (1.0 + e2)
    gates_ref[...] = jnp.where(lane == 0, w1, jnp.where(lane == 1, w2, 0.0))

    tm = x.shape[0]
    picked = ((lane == i1) | (lane == i2)).astype(BF16)
    before = (lax.broadcasted_iota(jnp.int32, (tm, tm), 1)
              < lax.broadcasted_iota(jnp.int32, (tm, tm), 0)).astype(BF16)
    prior = jnp.dot(before, picked, preferred_element_type=F32) + cnt_ref[...]
    r1 = jnp.sum(jnp.where(lane == i1, prior, 0.0), axis=-1, keepdims=True).astype(jnp.int32)
    r2 = jnp.sum(jnp.where(lane == i2, prior, 0.0), axis=-1, keepdims=True).astype(jnp.int32)
    cnt_ref[...] += jnp.sum(picked.astype(F32), axis=0, keepdims=True)
    packed = jnp.where(lane == 3, r2, 0)
    for at, val in ((2, r1), (1, i2), (0, i1)):
        packed = jnp.where(lane == at, val, packed)
    idx_ref[...] = packed


def _router(h, norm_g, router_w):
    t, d = h.shape
    tm = min(256, t)
    rw = jnp.zeros((d, LANES), F32).at[:, :N_EXPERTS].set(router_w.astype(F32))
    row = pl.BlockSpec((tm, d), lambda i: (i, 0))
    out = pl.BlockSpec((tm, LANES), lambda i: (i, 0))
    return pl.pallas_call(
        _router_kernel,
        out_shape=[jax.ShapeDtypeStruct((t, LANES), F32), jax.ShapeDtypeStruct((t, LANES), jnp.int32),
                   jax.ShapeDtypeStruct((1, LANES), F32)],
        grid=(t // tm,),
        in_specs=[row, pl.BlockSpec((1, d), lambda i: (0, 0)), pl.BlockSpec((d, LANES), lambda i: (0, 0))],
        out_specs=[out, out, pl.BlockSpec((1, LANES), lambda i: (0, 0))],
        compiler_params=_cparams(("arbitrary",)),
        name="moe_router",
    )(h, norm_g.reshape(1, d).astype(F32), rw)


def _row_copy(src_hbm, dst, sem, src_row, dst_row):
    return pltpu.make_async_copy(src_hbm.at[pl.ds(src_row, 1), :], dst.at[pl.ds(dst_row, 1), :], sem)


def _start_row_gather(idx_ref, src_hbm, dst, sem, rows):
    def start(r, carry):
        _row_copy(src_hbm, dst, sem, idx_ref[0, r], r).start()
        return carry

    lax.fori_loop(0, rows, start, 0, unroll=GATHER_UNROLL)


def _wait_row_gather(src_hbm, dst, sem, rows):
    def wait(r, carry):
        _row_copy(src_hbm, dst, sem, 0, r).wait()
        return carry

    lax.fori_loop(0, rows, wait, 0, unroll=GATHER_UNROLL)


def _prefetched_row_gather(idx_refs, nxt_refs, src_hbm, bufs, sems, rows):
    i = pl.program_id(0)
    slot = i % 2

    @pl.when(i == 0)
    def _():
        for idx_ref, buf, sem in zip(idx_refs, bufs, sems):
            _start_row_gather(idx_ref, src_hbm, buf.at[0], sem.at[0], rows)

    @pl.when(i + 1 < pl.num_programs(0))
    def _():
        for nxt_ref, buf, sem in zip(nxt_refs, bufs, sems):
            _start_row_gather(nxt_ref, src_hbm, buf.at[1 - slot], sem.at[1 - slot], rows)

    for buf, sem in zip(bufs, sems):
        _wait_row_gather(src_hbm, buf.at[slot], sem.at[slot], rows)
    return slot


def _idx_specs(n_steps, rows):
    cur = pl.BlockSpec((None, 1, rows), lambda i: (i, 0, 0), memory_space=pltpu.SMEM)
    nxt = pl.BlockSpec((None, 1, rows), lambda i: (jnp.minimum(i + 1, n_steps - 1), 0, 0),
                       memory_space=pltpu.SMEM)
    return cur, nxt


def _gather_norm_kernel(tok_ref, nxt_ref, h_hbm, g_ref, o_ref, buf, sem, *, rows):
    slot = _prefetched_row_gather([tok_ref], [nxt_ref], h_hbm, [buf], [sem], rows)
    g = g_ref[...]

    def norm(c, carry):
        r = pl.ds(pl.multiple_of(c * BF16_SUBLANES, BF16_SUBLANES), BF16_SUBLANES)
        x = buf[slot, r, :]
        y = x * lax.rsqrt(jnp.mean(x * x, axis=-1, keepdims=True) + EPS)
        o_ref[r, :] = (y * g).astype(o_ref.dtype)
        return carry

    lax.fori_loop(0, rows // BF16_SUBLANES, norm, 0, unroll=4)


def _gather_norm(h, norm_g, tok_of):
    t, d = h.shape
    p = tok_of.shape[0]
    rows = min(GATHER_ROWS, p)
    assert p % rows == 0
    n_steps = p // rows
    cur, nxt = _idx_specs(n_steps, rows)
    tok3 = tok_of.reshape(n_steps, 1, rows)
    return pl.pallas_call(
        functools.partial(_gather_norm_kernel, rows=rows),
        out_shape=jax.ShapeDtypeStruct((p, d), BF16),
        grid=(n_steps,),
        in_specs=[cur, nxt, pl.BlockSpec(memory_space=pl.ANY), pl.BlockSpec((1, d), lambda i: (0, 0))],
        out_specs=pl.BlockSpec((rows, d), lambda i: (i, 0)),
        scratch_shapes=[pltpu.VMEM((2, rows, d), F32), pltpu.SemaphoreType.DMA((2,))],
        compiler_params=_cparams(("arbitrary",)),
        name="moe_gather_norm",
    )(tok3, tok3, h, norm_g.reshape(1, d).astype(F32))


def _moe_gu_kernel(te_ref, nu_ref, x_ref, wg_ref, wu_ref, o_ref):
    @pl.when(pl.program_id(0) < nu_ref[0])
    def _():
        x = x_ref[...]
        g = jnp.dot(x, wg_ref[...], preferred_element_type=F32)
        u = jnp.dot(x, wu_ref[...], preferred_element_type=F32)
        o_ref[...] = (g * _sigmoid(g) * u).astype(o_ref.dtype)

    @pl.when(pl.program_id(0) >= nu_ref[0])
    def _():
        o_ref[...] = jnp.zeros_like(o_ref)


def _moe_gu(xs, w_gu, tile_expert, n_used, *, tn):
    p, k = xs.shape
    f = w_gu.shape[2] // 2
    tm = min(MOE_TM, p)
    tn = min(tn, f)
    nj = f // tn

    def wmap(off):
        return lambda t, j, te, nu: (te[t], 0, jnp.where(t < nu[0], j, 0) + off)

    grid_spec = pltpu.PrefetchScalarGridSpec(
        num_scalar_prefetch=2,
        grid=(p // tm, nj),
        in_specs=[
            pl.BlockSpec((tm, k), lambda t, j, te, nu: (t, 0)),
            pl.BlockSpec((None, k, tn), wmap(0)),
            pl.BlockSpec((None, k, tn), wmap(nj)),
        ],
        out_specs=pl.BlockSpec((tm, tn), lambda t, j, te, nu: (t, j)),
    )
    return pl.pallas_call(
        _moe_gu_kernel,
        out_shape=jax.ShapeDtypeStruct((p, f), BF16),
        grid_spec=grid_spec,
        compiler_params=_cparams(("arbitrary", "arbitrary")),
        name="moe_gate_up",
    )(tile_expert, n_used, xs, w_gu, w_gu)


def _moe_down_kernel(te_ref, nu_ref, x_ref, w_ref, o_ref):
    @pl.when(pl.program_id(0) < nu_ref[0])
    def _():
        o_ref[...] = jnp.dot(x_ref[...], w_ref[...], preferred_element_type=F32)

    @pl.when(pl.program_id(0) >= nu_ref[0])
    def _():
        o_ref[...] = jnp.zeros_like(o_ref)


def _moe_down(ha, w_down, tile_expert, n_used, *, tn):
    p, k = ha.shape
    n = w_down.shape[2]
    tm = min(MOE_TM, p)
    tn = min(tn, n)
    grid_spec = pltpu.PrefetchScalarGridSpec(
        num_scalar_prefetch=2,
        grid=(p // tm, n // tn),
        in_specs=[
            pl.BlockSpec((tm, k), lambda t, j, te, nu: (t, 0)),
            pl.BlockSpec((None, k, tn), lambda t, j, te, nu: (te[t], 0, jnp.where(t < nu[0], j, 0))),
        ],
        out_specs=pl.BlockSpec((tm, tn), lambda t, j, te, nu: (t, j)),
    )
    return pl.pallas_call(
        _moe_down_kernel,
        out_shape=jax.ShapeDtypeStruct((p, n), F32),
        grid_spec=grid_spec,
        compiler_params=_cparams(("arbitrary", "arbitrary")),
        name="moe_down",
    )(tile_expert, n_used, ha, w_down)


def _combine_kernel(p1_ref, n1_ref, p2_ref, n2_ref, h_ref, gates_ref, ys_hbm, o_ref, buf1, buf2, sem1, sem2, *,
                    rows):
    slot = _prefetched_row_gather([p1_ref, p2_ref], [n1_ref, n2_ref], ys_hbm, [buf1, buf2], [sem1, sem2], rows)
    gates = gates_ref[...]
    o_ref[...] = h_ref[...] + (gates[:, 0:1] * buf1[slot] + gates[:, 1:2] * buf2[slot])


def _moe_combine(h, gates, ys, pos1, pos2):
    t, d = h.shape
    rows = min(COMBINE_ROWS, t)
    n_steps = t // rows
    cur, nxt = _idx_specs(n_steps, rows)
    row = pl.BlockSpec((rows, d), lambda i: (i, 0))
    p1 = pos1.reshape(n_steps, 1, rows)
    p2 = pos2.reshape(n_steps, 1, rows)
    return pl.pallas_call(
        functools.partial(_combine_kernel, rows=rows),
        out_shape=jax.ShapeDtypeStruct((t, d), F32),
        grid=(n_steps,),
        in_specs=[cur, nxt, cur, nxt, row, pl.BlockSpec((rows, LANES), lambda i: (i, 0)),
                  pl.BlockSpec(memory_space=pl.ANY)],
        out_specs=row,
        scratch_shapes=[pltpu.VMEM((2, rows, d), F32), pltpu.VMEM((2, rows, d), F32),
                        pltpu.SemaphoreType.DMA((2,)), pltpu.SemaphoreType.DMA((2,))],
        compiler_params=_cparams(("arbitrary",)),
        name="moe_combine",
    )(p1, p1, p2, p2, h, gates, ys)


def _moe_plan(idx, counts, tm):
    t = idx.shape[0]
    counts = counts[0, :N_EXPERTS].astype(jnp.int32)
    padded = ((counts + tm - 1) // tm) * tm
    ends = jnp.cumsum(padded)
    starts = ends - padded
    experts = jnp.arange(N_EXPERTS, dtype=jnp.int32)[None, :]
    pos1 = jnp.sum(jnp.where(idx[:, 0:1] == experts, starts[None, :], 0), axis=1) + idx[:, 2]
    pos2 = jnp.sum(jnp.where(idx[:, 1:2] == experts, starts[None, :], 0), axis=1) + idx[:, 3]
    p_rows = 2 * t + N_EXPERTS * tm
    tok = jnp.arange(t, dtype=jnp.int32)
    tok_of = jnp.zeros((p_rows,), jnp.int32).at[jnp.concatenate([pos1, pos2])].set(jnp.concatenate([tok, tok]))
    n_tiles = p_rows // tm
    n_used = (ends[-1] // tm).astype(jnp.int32)
    tile_start = jnp.arange(n_tiles, dtype=jnp.int32) * tm
    tile_expert = jnp.sum((tile_start[:, None] >= ends[None, :]).astype(jnp.int32), axis=1)
    last_expert = jnp.sum(((ends[-1] - 1) >= ends).astype(jnp.int32))
    tile_expert = jnp.where(tile_start < ends[-1], tile_expert, last_expert).astype(jnp.int32)
    return tok_of, tile_expert, n_used.reshape(1), pos1.astype(jnp.int32), pos2.astype(jnp.int32)


def kernel(x, ret_norm_g, ret_w_in, ret_gn_g, ret_w_out, kv_norm_g, kv_w, k_norm_g, rel_bias, att_norm_g, att_w_q, q_norm_g, att_w_out, ffn_norm_g, ffn_w_gu, ffn_w_down, moe_norm_g, moe_router, moe_w_gu, moe_w_down):
    b, s, d = x.shape
    t = b * s
    ret_heads = d // RET_QK_DIM
    att_heads = d // ATT_HEAD_DIM
    att_dim = att_heads * ATT_HEAD_DIM
    h0 = x.reshape(t, d)

    n_exp, _, e_gu = moe_w_gu[0].shape

    (xn,) = _rmsnorm(h0, [ret_norm_g[0]], "ret_norm")
    proj, (ret_w_out_b, ffn_w_gu_b, ffn_w_down_b) = _matmul(
        xn, ret_w_in[0].astype(BF16), tm=1024, tn=1024, out_dtype=BF16, name="ret_in_proj",
        riders=(ret_w_out[0], ffn_w_gu[0], ffn_w_down[0]))
    y = _retention(proj.reshape(b, s, -1), ret_gn_g[0], b, s, ret_heads).reshape(t, -1)
    h1, (kv_w_b, att_w_q_b, att_w_out_b) = _matmul(
        y, ret_w_out_b, tm=1024, tn=512, out_dtype=F32, mode="res", res=h0, x_buffers=1, name="ret_out_proj",
        riders=(kv_w, att_w_q[0], att_w_out[0]))

    (xn,) = _rmsnorm(h1, [ffn_norm_g[0]], "ffn_norm")
    a, (moe_w_gu_b,) = _swiglu(xn, ffn_w_gu_b, tm=1024, tn=256, name="ffn_gate_up",
                               riders=(moe_w_gu[0].reshape(n_exp * d, e_gu),))
    h2, (moe_w_down_b,) = _matmul(a, ffn_w_down_b, tm=1024, tn=256, out_dtype=F32, mode="res", res=h1,
                                  x_buffers=1, name="ffn_down",
                                  riders=(moe_w_down[0].reshape(-1, d),))

    xkv, xq = _rmsnorm(h2, [kv_norm_g, att_norm_g[0]], "kv_att_norm")
    k, _ = _matmul(xkv, kv_w_b, tm=1024, tn=1024, out_dtype=BF16, mode="headnorm", gain=k_norm_g,
                   n_out=att_dim, name="k_proj")
    v, _ = _matmul(xkv, kv_w_b, tm=1024, tn=1024, out_dtype=BF16, n_out=att_dim, col0=att_dim, name="v_proj")
    q, _ = _matmul(xq, att_w_q_b, tm=1024, tn=1024, out_dtype=BF16, mode="headnorm",
                   gain=q_norm_g[0], scale=LOG2E * ATT_HEAD_DIM ** -0.5, name="q_proj")
    bias_win = _bias_windows(rel_bias, s)
    o = _moba_attention(q.reshape(b, s, att_dim), k.reshape(b, s, att_dim), v.reshape(b, s, att_dim),
                        bias_win, b, s, att_heads).reshape(t, att_dim)
    h3, _ = _matmul(o, att_w_out_b, tm=1024, tn=1024, out_dtype=F32, mode="res", res=h2, name="att_out_proj")

    gates, idx, counts = _router(h3, moe_norm_g[0], moe_router[0])
    tok_of, tile_expert, n_used, pos1, pos2 = _moe_plan(idx, counts, min(MOE_TM, 2 * t))
    xs = _gather_norm(h3, moe_norm_g[0], tok_of)
    ha = _moe_gu(xs, moe_w_gu_b.reshape(n_exp, d, e_gu), tile_expert, n_used, tn=1024)
    ys = _moe_down(ha, moe_w_down_b.reshape(n_exp, -1, d), tile_expert, n_used, tn=1024)
    out = _moe_combine(h3, gates, ys, pos1, pos2)
    return out.reshape(b, s, d)
```

```python
import functools
import math

import jax
import jax.numpy as jnp
from jax import lax
from jax.experimental import pallas as pl
from jax.experimental.pallas import tpu as pltpu

F32 = jnp.float32
BF16 = jnp.bfloat16

RET_QK_DIM = 256
RET_V_DIM = 512
ROPE_BASE = 10000.0
ATT_HEAD_DIM = 128
MOBA_BLOCK = 256
MOBA_TOPK = 3
REL_BUCKETS = 32
REL_MAX_DISTANCE = 4096
N_EXPERTS = 8
EPS = 1e-6

LANES = 128
BF16_SUBLANES = 16
V7X_VMEM_LIMIT_BYTES = 56 * 1024 * 1024
MASK_VALUE = -1e30

LOG2E = math.log2(math.e)

RET_CHUNK = 512
ATT_KEYS_PER_STEP = 1024
ATT_Q_PER_STEP = 1024
BIAS_ZERO = ATT_KEYS_PER_STEP - LANES
ONES_ROWS = 16
MOE_TM = 512
GATHER_ROWS = 256
COMBINE_ROWS = 128
GATHER_UNROLL = 8


def _cparams(sem):
    return pltpu.CompilerParams(dimension_semantics=sem, vmem_limit_bytes=V7X_VMEM_LIMIT_BYTES)


def _sigmoid(x):
    return 1.0 / (1.0 + jnp.exp(-x))


def _rmsnorm_kernel(x_ref, *refs):
    n = len(refs) // 2
    x = x_ref[...]
    y = x * lax.rsqrt(jnp.mean(x * x, axis=-1, keepdims=True) + EPS)
    for g_ref, o_ref in zip(refs[:n], refs[n:]):
        o_ref[...] = (y * g_ref[...]).astype(o_ref.dtype)


def _rmsnorm(x, gains, name):
    t, d = x.shape
    tm = min(256, t)
    n = len(gains)
    row = pl.BlockSpec((tm, d), lambda i: (i, 0))
    gspec = pl.BlockSpec((1, d), lambda i: (0, 0))
    outs = pl.pallas_call(
        _rmsnorm_kernel,
        out_shape=[jax.ShapeDtypeStruct((t, d), BF16)] * n,
        grid=(t // tm,),
        in_specs=[row] + [gspec] * n,
        out_specs=[row] * n,
        compiler_params=_cparams(("arbitrary",)),
        name=name,
    )(x, *[g.reshape(1, d).astype(F32) for g in gains])
    return outs


def _cast_riders(riders, grid):
    steps = grid[0] * grid[1]
    specs, shapes = [], []
    for a in riders:
        rows, cols = a.shape
        nblk = next(n for n in range(min(steps, rows // BF16_SUBLANES), 0, -1)
                    if rows % n == 0 and (rows // n) % BF16_SUBLANES == 0)
        specs.append(pl.BlockSpec((rows // nblk, cols),
                                  lambda i, j, nblk=nblk: (jnp.minimum(i * grid[1] + j, nblk - 1), 0)))
        shapes.append(jax.ShapeDtypeStruct((rows, cols), BF16))
    return specs, shapes


def _run_riders(ins, outs):
    for src, dst in zip(ins, outs):
        dst[...] = src[...].astype(dst.dtype)


def _mm_kernel(x_ref, w_ref, *refs, mode, scale, n_riders):
    n_extra = 0 if mode == "plain" else 1
    extra = refs[:n_extra]
    o_ref = refs[n_extra + n_riders]
    _run_riders(refs[n_extra:n_extra + n_riders], refs[n_extra + n_riders + 1:])
    acc = jnp.dot(x_ref[...], w_ref[...], preferred_element_type=F32)
    if mode == "plain":
        o_ref[...] = acc.astype(o_ref.dtype)
    elif mode == "res":
        o_ref[...] = extra[0][...] + acc
    elif mode == "headnorm":
        g = extra[0][...] * scale
        for c in range(acc.shape[1] // LANES):
            a = acc[:, c * LANES:(c + 1) * LANES]
            y = a * lax.rsqrt(jnp.mean(a * a, axis=-1, keepdims=True) + EPS)
            o_ref[:, c * LANES:(c + 1) * LANES] = (y * g).astype(o_ref.dtype)
    else:
        raise ValueError(mode)


def _matmul(x, w, *, tm, tn, out_dtype, name, mode="plain", res=None, gain=None, scale=1.0, x_buffers=2,
            n_out=None, col0=0, riders=()):
    m, k = x.shape
    n = w.shape[1] if n_out is None else n_out
    tm, tn = min(tm, m), min(tn, n)
    assert m % tm == 0 and n % tn == 0 and col0 % tn == 0, (m, n, tm, tn, col0)
    grid = (m // tm, n // tn)
    jb = col0 // tn
    xspec = pl.BlockSpec((tm, k), lambda i, j: (i, 0), pipeline_mode=pl.Buffered(x_buffers))
    in_specs = [xspec, pl.BlockSpec((k, tn), lambda i, j: (0, j + jb))]
    args = [x, w]
    if mode == "res":
        in_specs.append(pl.BlockSpec((tm, tn), lambda i, j: (i, j)))
        args.append(res)
    elif mode == "headnorm":
        in_specs.append(pl.BlockSpec((1, LANES), lambda i, j: (0, 0)))
        args.append(gain.reshape(1, LANES).astype(F32))
    rider_specs, rider_shapes = _cast_riders(riders, grid)
    outs = pl.pallas_call(
        functools.partial(_mm_kernel, mode=mode, scale=scale, n_riders=len(riders)),
        out_shape=[jax.ShapeDtypeStruct((m, n), out_dtype)] + rider_shapes,
        grid=grid,
        in_specs=in_specs + rider_specs,
        out_specs=[pl.BlockSpec((tm, tn), lambda i, j: (i, j))] + rider_specs,
        compiler_params=_cparams(("arbitrary", "arbitrary")),
        name=name,
    )(*args, *riders)
    return outs[0], outs[1:]


def _swiglu_kernel(x_ref, wg_ref, wu_ref, *refs, n_riders):
    o_ref = refs[n_riders]
    _run_riders(refs[:n_riders], refs[n_riders + 1:])
    x = x_ref[...]
    g = jnp.dot(x, wg_ref[...], preferred_element_type=F32)
    u = jnp.dot(x, wu_ref[...], preferred_element_type=F32)
    o_ref[...] = (g * _sigmoid(g) * u).astype(o_ref.dtype)


def _swiglu(x, w_gu, *, tm, tn, name, riders=()):
    m, k = x.shape
    f = w_gu.shape[1] // 2
    tm, tn = min(tm, m), min(tn, f)
    assert m % tm == 0 and f % tn == 0
    nj = f // tn
    grid = (m // tm, nj)
    rider_specs, rider_shapes = _cast_riders(riders, grid)
    outs = pl.pallas_call(
        functools.partial(_swiglu_kernel, n_riders=len(riders)),
        out_shape=[jax.ShapeDtypeStruct((m, f), BF16)] + rider_shapes,
        grid=grid,
        in_specs=[
            pl.BlockSpec((tm, k), lambda i, j: (i, 0)),
            pl.BlockSpec((k, tn), lambda i, j: (0, j)),
            pl.BlockSpec((k, tn), lambda i, j: (0, j + nj)),
        ] + rider_specs,
        out_specs=[pl.BlockSpec((tm, tn), lambda i, j: (i, j))] + rider_specs,
        compiler_params=_cparams(("arbitrary", "arbitrary")),
        name=name,
    )(x, w_gu, w_gu, *riders)
    return outs[0], outs[1:]


def _retention_kernel(lg_ref, q_ref, k_ref, v_ref, g_ref, cos_ref, sin_ref, gn_ref, o_ref, state_ref, intra_ref,
                      dq_ref, dk_ref, *, chunk):
    h = pl.program_id(1)
    c = pl.program_id(2)
    lg = lg_ref[h]

    @pl.when(c == 0)
    def _():
        state_ref[...] = jnp.zeros_like(state_ref)
        row = lax.broadcasted_iota(jnp.int32, (chunk, 1), 0).astype(F32)
        col = lax.broadcasted_iota(jnp.int32, (1, chunk), 1).astype(F32)
        diff = row - col
        intra_ref[...] = jnp.where(diff >= 0, jnp.exp(lg * jnp.maximum(diff, 0.0)), 0.0)
        dq_ref[...] = jnp.broadcast_to(jnp.exp(lg * (row + 1.0)), dq_ref.shape)
        dk_ref[...] = jnp.broadcast_to(jnp.exp(lg * (chunk - 1.0 - row)), dk_ref.shape)

    cos = cos_ref[...]
    sin = sin_ref[...]
    half = RET_QK_DIM // 2

    def rot(x):
        x1, x2 = x[:, :half], x[:, half:]
        return jnp.concatenate([x1 * cos - x2 * sin, x1 * sin + x2 * cos], axis=-1)

    q = rot(q_ref[...].astype(F32))
    k = rot(k_ref[...].astype(F32)) * (RET_QK_DIM ** -0.5)
    v = v_ref[...]

    decay_chunk = jnp.exp(jnp.full((1, 1), chunk, F32) * lg)

    qb = q.astype(BF16)
    scores = lax.dot_general(qb, k.astype(BF16), (((1,), (1,)), ((), ())), preferred_element_type=F32)
    scores = scores * intra_ref[...]
    inner = jnp.dot(scores.astype(BF16), v, preferred_element_type=F32)
    state = state_ref[...]
    decay_q = jnp.concatenate([dq_ref[...]] * (RET_V_DIM // LANES), axis=-1)
    cross = jnp.dot(qb, state.astype(BF16), preferred_element_type=F32) * decay_q
    kd = (k * jnp.concatenate([dk_ref[...]] * (RET_QK_DIM // LANES), axis=-1)).astype(BF16)
    state_ref[...] = state * decay_chunk + lax.dot_general(
        kd, v, (((0,), (0,)), ((), ())), preferred_element_type=F32)

    o = inner + cross
    mu = jnp.mean(o, axis=-1, keepdims=True)
    oc = o - mu
    var = jnp.mean(oc * oc, axis=-1, keepdims=True)
    on = oc * lax.rsqrt(var + EPS) * gn_ref[...]
    gate = g_ref[...].astype(F32)
    o_ref[...] = (gate * _sigmoid(gate) * on).astype(o_ref.dtype)


def _retention(proj, gn_g, b, s, heads):
    dk, dv = RET_QK_DIM, RET_V_DIM
    chunk = min(RET_CHUNK, s)
    assert s % chunk == 0
    pos = jnp.arange(s, dtype=F32)
    inv = 1.0 / (ROPE_BASE ** jnp.linspace(0.0, 1.0, dk // 2, dtype=F32))
    ang = pos[:, None] * inv[None, :]
    cos, sin = jnp.cos(ang), jnp.sin(ang)
    log_gamma = jnp.log1p(-jnp.exp2(-5.0 - jnp.arange(heads, dtype=F32)))
    k_off = heads
    v_off = (2 * heads * dk) // dv
    g_off = v_off + heads
    grid_spec = pltpu.PrefetchScalarGridSpec(
        num_scalar_prefetch=1,
        grid=(b, heads, s // chunk),
        in_specs=[
            pl.BlockSpec((None, chunk, dk), lambda bi, h, c, lg: (bi, c, h)),
            pl.BlockSpec((None, chunk, dk), lambda bi, h, c, lg: (bi, c, k_off + h)),
            pl.BlockSpec((None, chunk, dv), lambda bi, h, c, lg: (bi, c, v_off + h)),
            pl.BlockSpec((None, chunk, dv), lambda bi, h, c, lg: (bi, c, g_off + h)),
            pl.BlockSpec((chunk, dk // 2), lambda bi, h, c, lg: (c, 0)),
            pl.BlockSpec((chunk, dk // 2), lambda bi, h, c, lg: (c, 0)),
            pl.BlockSpec((1, dv), lambda bi, h, c, lg: (0, h)),
        ],
        out_specs=pl.BlockSpec((None, chunk, dv), lambda bi, h, c, lg: (bi, c, h)),
        scratch_shapes=[
            pltpu.VMEM((dk, dv), F32),
            pltpu.VMEM((chunk, chunk), F32),
            pltpu.VMEM((chunk, LANES), F32),
            pltpu.VMEM((chunk, LANES), F32),
        ],
    )
    return pl.pallas_call(
        functools.partial(_retention_kernel, chunk=chunk),
        out_shape=jax.ShapeDtypeStruct((b, s, heads * dv), BF16),
        grid_spec=grid_spec,
        compiler_params=_cparams(("arbitrary", "arbitrary", "arbitrary")),
        name="retention",
    )(log_gamma, proj, proj, proj, proj, cos, sin, gn_g.reshape(1, heads * dv).astype(F32))


def _rel_bucket(dist):
    n = jnp.maximum(dist, 0)
    max_exact = REL_BUCKETS // 2
    nf = jnp.maximum(n, max_exact).astype(F32)
    large = max_exact + (jnp.log(nf / max_exact) / math.log(REL_MAX_DISTANCE / max_exact)
                         * (REL_BUCKETS - max_exact)).astype(jnp.int32)
    large = jnp.minimum(large, REL_BUCKETS - 1)
    return jnp.where(n < max_exact, n, large)


def _bias_windows(rel_bias, s):
    width = s + BIAS_ZERO
    heads = rel_bias.shape[1]
    u = jnp.arange(width + LANES)
    dist = u - (BIAS_ZERO + LANES - 1)
    tab = jnp.where(dist[None, :] >= 0, LOG2E * rel_bias.astype(F32).T[:, _rel_bucket(dist)], MASK_VALUE)
    return pl.pallas_call(
        _bias_window_kernel,
        out_shape=jax.ShapeDtypeStruct((heads, LANES, width), F32),
        grid=(heads,),
        in_specs=[pl.BlockSpec((None, 1, width + LANES), lambda h: (h, 0, 0))],
        out_specs=pl.BlockSpec((None, LANES, width), lambda h: (h, 0, 0)),
        compiler_params=_cparams(("arbitrary",)),
        name="bias_windows",
    )(tab.reshape(heads, 1, width + LANES))


def _bias_window_kernel(tab_ref, o_ref):
    w = tab_ref.shape[-1]
    x = jnp.broadcast_to(tab_ref[...], (LANES, w))
    shifted = pltpu.roll(x, w - (LANES - 1), 1, stride=1, stride_axis=0)
    o_ref[...] = shifted[:, :o_ref.shape[-1]]


def _moba_kernel(q_ref, k_ref, v_ref, bias_ref, o_ref, ka_ref, vt_ref, km_ref, s_ref, *, seq):
    blk, dh, gk = MOBA_BLOCK, ATT_HEAD_DIM, ATT_KEYS_PER_STEP
    nb = seq // blk
    nq = q_ref.shape[0]
    q0 = pl.program_id(2) * nq

    @pl.when(q0 == 0)
    def _():
        km_ref[...] = jnp.zeros_like(km_ref)
        lane = lax.broadcasted_iota(jnp.int32, (blk, LANES), 1)

        def fill(n, carry):
            rows = pl.ds(pl.multiple_of(n * blk, blk), blk)
            kb = k_ref[rows, :]
            km_ref[pl.ds(n, 1), :] = jnp.mean(kb.astype(F32), axis=0, keepdims=True)
            ka_ref[rows, 0:dh] = kb
            ka_ref[rows, dh:dh + LANES] = (lane == n).astype(BF16)
            vt_ref[0:dh, rows] = v_ref[rows, :].astype(F32).T.astype(BF16)
            vt_ref[dh:dh + ONES_ROWS, rows] = jnp.ones((ONES_ROWS, blk), BF16)
            return carry

        lax.fori_loop(0, nb, fill, 0)

    qt = q_ref[...].astype(F32).T.astype(BF16)

    km = km_ref[...]
    km_hi = km.astype(BF16)
    km_lo = (km - km_hi.astype(F32)).astype(BF16)
    gate = (jnp.dot(km_hi, qt, preferred_element_type=F32)
            + jnp.dot(km_lo, qt, preferred_element_type=F32))
    bid = lax.broadcasted_iota(jnp.int32, (LANES, nq), 0)
    own = (q0 + lax.broadcasted_iota(jnp.int32, (LANES, nq), 1)) // blk
    past = bid < own
    gate = jnp.where(past, gate, -jnp.inf)
    chosen = bid == own
    for _ in range(MOBA_TOPK):
        top = jnp.max(gate, axis=0, keepdims=True)
        idx = jnp.min(jnp.where(gate == top, bid, LANES), axis=0, keepdims=True)
        pick = bid == idx
        chosen = chosen | (pick & past)
        gate = jnp.where(pick, -jnp.inf, gate)
    penalty = jnp.where(chosen, 0.0, MASK_VALUE).astype(BF16)
    qa = jnp.concatenate([qt, penalty], axis=0)

    halves = 4
    hk = gk // halves

    def score(g):
        k0 = g * gk
        m0 = BIAS_ZERO + q0 - k0
        top = None
        for t in range(halves):
            keys = pl.ds(pl.multiple_of(k0 + t * hk, hk), hk)
            bias = jnp.concatenate(
                [bias_ref[:, pl.ds(pl.multiple_of(m0 - t * hk - LANES * a, LANES), nq)]
                 for a in range(hk // LANES)], axis=0)
            st = jnp.dot(ka_ref[keys, :], qa, preferred_element_type=F32) + bias
            s_ref[t * hk:(t + 1) * hk, 0:nq] = st
            mt = jnp.max(st, axis=0, keepdims=True)
            top = mt if top is None else jnp.maximum(top, mt)
        return top

    def absorb(g, top, m_prev, acc):
        m_new = jnp.maximum(m_prev, top)
        acc = jnp.exp2(m_prev - m_new) * acc
        for t in range(halves):
            keys = pl.ds(pl.multiple_of(g * gk + t * hk, hk), hk)
            p = jnp.exp2(s_ref[t * hk:(t + 1) * hk, 0:nq] - m_new)
            acc = acc + jnp.dot(vt_ref[:, keys], p.astype(BF16), preferred_element_type=F32)
        return m_new, acc

    def step(g, carry):
        top, m_prev, acc = carry
        m_new, acc = absorb(g, top, m_prev, acc)
        return score(g + 1), m_new, acc

    last = q0 // gk
    init = (score(0), jnp.full((1, nq), MASK_VALUE, F32), jnp.zeros((dh + ONES_ROWS, nq), F32))
    top, m_prev, acc = lax.fori_loop(0, last, step, init)
    _, acc = absorb(last, top, m_prev, acc)
    o_ref[...] = (acc[0:dh, :] / acc[dh:dh + 1, :]).T.astype(o_ref.dtype)


def _moba_attention(q, k, v, bias_win, b, s, heads):
    dh, blk = ATT_HEAD_DIM, MOBA_BLOCK
    nq = min(ATT_Q_PER_STEP, s)
    assert dh == LANES and s % ATT_KEYS_PER_STEP == 0 and s // blk <= LANES
    assert ATT_KEYS_PER_STEP % nq == 0 and nq % blk == 0
    qspec = pl.BlockSpec((None, nq, dh), lambda bi, h, i: (bi, i, h))
    kvspec = pl.BlockSpec((None, s, dh), lambda bi, h, i: (bi, 0, h))
    return pl.pallas_call(
        functools.partial(_moba_kernel, seq=s),
        out_shape=jax.ShapeDtypeStruct((b, s, heads * dh), BF16),
        grid=(b, heads, s // nq),
        in_specs=[qspec, kvspec, kvspec,
                  pl.BlockSpec((None, LANES, s + BIAS_ZERO), lambda bi, h, i: (h, 0, 0))],
        out_specs=qspec,
        scratch_shapes=[
            pltpu.VMEM((s, dh + LANES), BF16),
            pltpu.VMEM((dh + ONES_ROWS, s), BF16),
            pltpu.VMEM((LANES, dh), F32),
            pltpu.VMEM((ATT_KEYS_PER_STEP, nq + LANES), F32),
        ],
        compiler_params=_cparams(("arbitrary", "arbitrary", "arbitrary")),
        name="moba_attention",
    )(q, k, v, bias_win)


def _router_kernel(x_ref, g_ref, rw_ref, gates_ref, idx_ref, cnt_ref):
    @pl.when(pl.program_id(0) == 0)
    def _():
        cnt_ref[...] = jnp.zeros_like(cnt_ref)

    x = x_ref[...]
    xn = x * lax.rsqrt(jnp.mean(x * x, axis=-1, keepdims=True) + EPS) * g_ref[...]
    xn_hi = xn.astype(BF16)
    xn_lo = (xn - xn_hi.astype(F32)).astype(BF16)
    rw = rw_ref[...]
    both = jnp.dot(xn_hi, rw, preferred_element_type=F32)
    logits = (both[:, :LANES] + both[:, LANES:]) + jnp.dot(xn_lo, rw[:, :LANES], preferred_element_type=F32)
    lane = lax.broadcasted_iota(jnp.int32, logits.shape, 1)
    logits = jnp.where(lane < N_EXPERTS, logits, -jnp.inf)
    v1 = jnp.max(logits, axis=-1, keepdims=True)
    i1 = jnp.min(jnp.where(logits == v1, lane, LANES), axis=-1, keepdims=True)
    rest = jnp.where(lane == i1, -jnp.inf, logits)
    v2 = jnp.max(rest, axis=-1, keepdims=True)
    i2 = jnp.min(jnp.where(rest == v2, lane, LANES), axis=-1, keepdims=True)
    e2 = jnp.exp(v2 - v1)
    w1 = 1.0 / (1.0 + e2)
    w2 = e2 / (1.0 + e2)
    gates_ref[...] = jnp.where(lane == 0, w1, jnp.where(lane == 1, w2, 0.0))

    tm = x.shape[0]
    picked = ((lane == i1) | (lane == i2)).astype(BF16)
    before = (lax.broadcasted_iota(jnp.int32, (tm, tm), 1)
              < lax.broadcasted_iota(jnp.int32, (tm, tm), 0)).astype(BF16)
    prior = jnp.dot(before, picked, preferred_element_type=F32) + cnt_ref[...]
    r1 = jnp.sum(jnp.where(lane == i1, prior, 0.0), axis=-1, keepdims=True).astype(jnp.int32)
    r2 = jnp.sum(jnp.where(lane == i2, prior, 0.0), axis=-1, keepdims=True).astype(jnp.int32)
    cnt_ref[...] += jnp.sum(picked.astype(F32), axis=0, keepdims=True)
    packed = jnp.where(lane == 3, r2, 0)
    for at, val in ((2, r1), (1, i2), (0, i1)):
        packed = jnp.where(lane == at, val, packed)
    idx_ref[...] = packed


def _router(h, norm_g, router_w):
    t, d = h.shape
    tm = min(256, t)
    rw = jnp.zeros((d, LANES), F32).at[:, :N_EXPERTS].set(router_w.astype(F32))
    rw_hi = rw.astype(BF16)
    rw = jnp.concatenate([rw_hi, (rw - rw_hi.astype(F32)).astype(BF16)], axis=1)
    row = pl.BlockSpec((tm, d), lambda i: (i, 0))
    out = pl.BlockSpec((tm, LANES), lambda i: (i, 0))
    return pl.pallas_call(
        _router_kernel,
        out_shape=[jax.ShapeDtypeStruct((t, LANES), F32), jax.ShapeDtypeStruct((t, LANES), jnp.int32),
                   jax.ShapeDtypeStruct((1, LANES), F32)],
        grid=(t // tm,),
        in_specs=[row, pl.BlockSpec((1, d), lambda i: (0, 0)), pl.BlockSpec((d, 2 * LANES), lambda i: (0, 0))],
        out_specs=[out, out, pl.BlockSpec((1, LANES), lambda i: (0, 0))],
        compiler_params=_cparams(("arbitrary",)),
        name="moe_router",
    )(h, norm_g.reshape(1, d).astype(F32), rw)


def _row_copy(src_hbm, dst, sem, src_row, dst_row):
    return pltpu.make_async_copy(src_hbm.at[pl.ds(src_row, 1), :], dst.at[pl.ds(dst_row, 1), :], sem)


def _start_row_gather(idx_ref, src_hbm, dst, sem, rows):
    def start(r, carry):
        _row_copy(src_hbm, dst, sem, idx_ref[0, r], r).start()
        return carry

    lax.fori_loop(0, rows, start, 0, unroll=GATHER_UNROLL)


def _wait_row_gather(src_hbm, dst, sem, rows):
    def wait(r, carry):
        _row_copy(src_hbm, dst, sem, 0, r).wait()
        return carry

    lax.fori_loop(0, rows, wait, 0, unroll=GATHER_UNROLL)


def _prefetched_row_gather(idx_refs, nxt_refs, src_hbm, bufs, sems, rows):
    i = pl.program_id(0)
    slot = i % 2

    @pl.when(i == 0)
    def _():
        for idx_ref, buf, sem in zip(idx_refs, bufs, sems):
            _start_row_gather(idx_ref, src_hbm, buf.at[0], sem.at[0], rows)

    @pl.when(i + 1 < pl.num_programs(0))
    def _():
        for nxt_ref, buf, sem in zip(nxt_refs, bufs, sems):
            _start_row_gather(nxt_ref, src_hbm, buf.at[1 - slot], sem.at[1 - slot], rows)

    for buf, sem in zip(bufs, sems):
        _wait_row_gather(src_hbm, buf.at[slot], sem.at[slot], rows)
    return slot


def _idx_specs(n_steps, rows):
    cur = pl.BlockSpec((None, 1, rows), lambda i: (i, 0, 0), memory_space=pltpu.SMEM)
    nxt = pl.BlockSpec((None, 1, rows), lambda i: (jnp.minimum(i + 1, n_steps - 1), 0, 0),
                       memory_space=pltpu.SMEM)
    return cur, nxt


def _gather_norm_kernel(tok_ref, nxt_ref, h_hbm, g_ref, o_ref, buf, sem, *, rows):
    slot = _prefetched_row_gather([tok_ref], [nxt_ref], h_hbm, [buf], [sem], rows)
    g = g_ref[...]

    def norm(c, carry):
        r = pl.ds(pl.multiple_of(c * BF16_SUBLANES, BF16_SUBLANES), BF16_SUBLANES)
        x = buf[slot, r, :]
        y = x * lax.rsqrt(jnp.mean(x * x, axis=-1, keepdims=True) + EPS)
        o_ref[r, :] = (y * g).astype(o_ref.dtype)
        return carry

    lax.fori_loop(0, rows // BF16_SUBLANES, norm, 0, unroll=4)


def _gather_norm(h, norm_g, tok_of):
    t, d = h.shape
    p = tok_of.shape[0]
    rows = min(GATHER_ROWS, p)
    assert p % rows == 0
    n_steps = p // rows
    cur, nxt = _idx_specs(n_steps, rows)
    tok3 = tok_of.reshape(n_steps, 1, rows)
    return pl.pallas_call(
        functools.partial(_gather_norm_kernel, rows=rows),
        out_shape=jax.ShapeDtypeStruct((p, d), BF16),
        grid=(n_steps,),
        in_specs=[cur, nxt, pl.BlockSpec(memory_space=pl.ANY), pl.BlockSpec((1, d), lambda i: (0, 0))],
        out_specs=pl.BlockSpec((rows, d), lambda i: (i, 0)),
        scratch_shapes=[pltpu.VMEM((2, rows, d), F32), pltpu.SemaphoreType.DMA((2,))],
        compiler_params=_cparams(("arbitrary",)),
        name="moe_gather_norm",
    )(tok3, tok3, h, norm_g.reshape(1, d).astype(F32))


def _moe_gu_kernel(te_ref, nu_ref, x_ref, wg_ref, wu_ref, o_ref):
    @pl.when(pl.program_id(0) < nu_ref[0])
    def _():
        x = x_ref[...]
        g = jnp.dot(x, wg_ref[...], preferred_element_type=F32)
        u = jnp.dot(x, wu_ref[...], preferred_element_type=F32)
        o_ref[...] = (g * _sigmoid(g) * u).astype(o_ref.dtype)

    @pl.when(pl.program_id(0) >= nu_ref[0])
    def _():
        o_ref[...] = jnp.zeros_like(o_ref)


def _moe_gu(xs, w_gu, tile_expert, n_used, *, tn):
    p, k = xs.shape
    f = w_gu.shape[2] // 2
    tm = min(MOE_TM, p)
    tn = min(tn, f)
    nj = f // tn

    def wmap(off):
        return lambda t, j, te, nu: (te[t], 0, jnp.where(t < nu[0], j, 0) + off)

    grid_spec = pltpu.PrefetchScalarGridSpec(
        num_scalar_prefetch=2,
        grid=(p // tm, nj),
        in_specs=[
            pl.BlockSpec((tm, k), lambda t, j, te, nu: (t, 0)),
            pl.BlockSpec((None, k, tn), wmap(0)),
            pl.BlockSpec((None, k, tn), wmap(nj)),
        ],
        out_specs=pl.BlockSpec((tm, tn), lambda t, j, te, nu: (t, j)),
    )
    return pl.pallas_call(
        _moe_gu_kernel,
        out_shape=jax.ShapeDtypeStruct((p, f), BF16),
        grid_spec=grid_spec,
        compiler_params=_cparams(("arbitrary", "arbitrary")),
        name="moe_gate_up",
    )(tile_expert, n_used, xs, w_gu, w_gu)


def _moe_down_kernel(te_ref, nu_ref, x_ref, w_ref, o_ref):
    @pl.when(pl.program_id(0) < nu_ref[0])
    def _():
        o_ref[...] = jnp.dot(x_ref[...], w_ref[...], preferred_element_type=F32)

    @pl.when(pl.program_id(0) >= nu_ref[0])
    def _():
        o_ref[...] = jnp.zeros_like(o_ref)


def _moe_down(ha, w_down, tile_expert, n_used, *, tn):
    p, k = ha.shape
    n = w_down.shape[2]
    tm = min(MOE_TM, p)
    tn = min(tn, n)
    grid_spec = pltpu.PrefetchScalarGridSpec(
        num_scalar_prefetch=2,
        grid=(p // tm, n // tn),
        in_specs=[
            pl.BlockSpec((tm, k), lambda t, j, te, nu: (t, 0)),
            pl.BlockSpec((None, k, tn), lambda t, j, te, nu: (te[t], 0, jnp.where(t < nu[0], j, 0))),
        ],
        out_specs=pl.BlockSpec((tm, tn), lambda t, j, te, nu: (t, j)),
    )
    return pl.pallas_call(
        _moe_down_kernel,
        out_shape=jax.ShapeDtypeStruct((p, n), F32),
        grid_spec=grid_spec,
        compiler_params=_cparams(("arbitrary", "arbitrary")),
        name="moe_down",
    )(tile_expert, n_used, ha, w_down)


def _combine_kernel(p1_ref, n1_ref, p2_ref, n2_ref, h_ref, gates_ref, ys_hbm, o_ref, buf1, buf2, sem1, sem2, *,
                    rows):
    slot = _prefetched_row_gather([p1_ref, p2_ref], [n1_ref, n2_ref], ys_hbm, [buf1, buf2], [sem1, sem2], rows)
    gates = gates_ref[...]
    o_ref[...] = h_ref[...] + (gates[:, 0:1] * buf1[slot] + gates[:, 1:2] * buf2[slot])


def _moe_combine(h, gates, ys, pos1, pos2):
    t, d = h.shape
    rows = min(COMBINE_ROWS, t)
    n_steps = t // rows
    cur, nxt = _idx_specs(n_steps, rows)
    row = pl.BlockSpec((rows, d), lambda i: (i, 0))
    p1 = pos1.reshape(n_steps, 1, rows)
    p2 = pos2.reshape(n_steps, 1, rows)
    return pl.pallas_call(
        functools.partial(_combine_kernel, rows=rows),
        out_shape=jax.ShapeDtypeStruct((t, d), F32),
        grid=(n_steps,),
        in_specs=[cur, nxt, cur, nxt, row, pl.BlockSpec((rows, LANES), lambda i: (i, 0)),
                  pl.BlockSpec(memory_space=pl.ANY)],
        out_specs=row,
        scratch_shapes=[pltpu.VMEM((2, rows, d), F32), pltpu.VMEM((2, rows, d), F32),
                        pltpu.SemaphoreType.DMA((2,)), pltpu.SemaphoreType.DMA((2,))],
        compiler_params=_cparams(("arbitrary",)),
        name="moe_combine",
    )(p1, p1, p2, p2, h, gates, ys)


def _moe_plan(idx, counts, tm):
    t = idx.shape[0]
    counts = counts[0, :N_EXPERTS].astype(jnp.int32)
    padded = ((counts + tm - 1) // tm) * tm
    ends = jnp.cumsum(padded)
    starts = ends - padded
    experts = jnp.arange(N_EXPERTS, dtype=jnp.int32)[None, :]
    pos1 = jnp.sum(jnp.where(idx[:, 0:1] == experts, starts[None, :], 0), axis=1) + idx[:, 2]
    pos2 = jnp.sum(jnp.where(idx[:, 1:2] == experts, starts[None, :], 0), axis=1) + idx[:, 3]
    p_rows = 2 * t + N_EXPERTS * tm
    tok = jnp.arange(t, dtype=jnp.int32)
    tok_of = jnp.zeros((p_rows,), jnp.int32).at[jnp.concatenate([pos1, pos2])].set(jnp.concatenate([tok, tok]))
    n_tiles = p_rows // tm
    n_used = (ends[-1] // tm).astype(jnp.int32)
    tile_start = jnp.arange(n_tiles, dtype=jnp.int32) * tm
    tile_expert = jnp.sum((tile_start[:, None] >= ends[None, :]).astype(jnp.int32), axis=1)
    last_expert = jnp.sum(((ends[-1] - 1) >= ends).astype(jnp.int32))
    tile_expert = jnp.where(tile_start < ends[-1], tile_expert, last_expert).astype(jnp.int32)
    return tok_of, tile_expert, n_used.reshape(1), pos1.astype(jnp.int32), pos2.astype(jnp.int32)


def kernel(x, ret_norm_g, ret_w_in, ret_gn_g, ret_w_out, kv_norm_g, kv_w, k_norm_g, rel_bias, att_norm_g, att_w_q, q_norm_g, att_w_out, ffn_norm_g, ffn_w_gu, ffn_w_down, moe_norm_g, moe_router, moe_w_gu, moe_w_down):
    b, s, d = x.shape
    t = b * s
    ret_heads = d // RET_QK_DIM
    att_heads = d // ATT_HEAD_DIM
    att_dim = att_heads * ATT_HEAD_DIM
    h0 = x.reshape(t, d)

    n_exp, _, e_gu = moe_w_gu[0].shape

    (xn,) = _rmsnorm(h0, [ret_norm_g[0]], "ret_norm")
    proj, (ret_w_out_b, ffn_w_gu_b, ffn_w_down_b) = _matmul(
        xn, ret_w_in[0].astype(BF16), tm=1024, tn=1024, out_dtype=BF16, name="ret_in_proj",
        riders=(ret_w_out[0], ffn_w_gu[0], ffn_w_down[0]))
    y = _retention(proj.reshape(b, s, -1), ret_gn_g[0], b, s, ret_heads).reshape(t, -1)
    h1, (kv_w_b, att_w_q_b, att_w_out_b) = _matmul(
        y, ret_w_out_b, tm=1024, tn=512, out_dtype=F32, mode="res", res=h0, x_buffers=1, name="ret_out_proj",
        riders=(kv_w, att_w_q[0], att_w_out[0]))

    (xn,) = _rmsnorm(h1, [ffn_norm_g[0]], "ffn_norm")
    a, (moe_w_gu_b,) = _swiglu(xn, ffn_w_gu_b, tm=1024, tn=256, name="ffn_gate_up",
                               riders=(moe_w_gu[0].reshape(n_exp * d, e_gu),))
    h2, (moe_w_down_b,) = _matmul(a, ffn_w_down_b, tm=1024, tn=256, out_dtype=F32, mode="res", res=h1,
                                  x_buffers=1, name="ffn_down",
                                  riders=(moe_w_down[0].reshape(-1, d),))

    xkv, xq = _rmsnorm(h2, [kv_norm_g, att_norm_g[0]], "kv_att_norm")
    k, _ = _matmul(xkv, kv_w_b, tm=1024, tn=1024, out_dtype=BF16, mode="headnorm", gain=k_norm_g,
                   n_out=att_dim, name="k_proj")
    v, _ = _matmul(xkv, kv_w_b, tm=1024, tn=1024, out_dtype=BF16, n_out=att_dim, col0=att_dim, name="v_proj")
    q, _ = _matmul(xq, att_w_q_b, tm=1024, tn=1024, out_dtype=BF16, mode="headnorm",
                   gain=q_norm_g[0], scale=LOG2E * ATT_HEAD_DIM ** -0.5, name="q_proj")
    bias_win = _bias_windows(rel_bias, s)
    o = _moba_attention(q.reshape(b, s, att_dim), k.reshape(b, s, att_dim), v.reshape(b, s, att_dim),
                        bias_win, b, s, att_heads).reshape(t, att_dim)
    h3, _ = _matmul(o, att_w_out_b, tm=1024, tn=1024, out_dtype=F32, mode="res", res=h2, name="att_out_proj")

    gates, idx, counts = _router(h3, moe_norm_g[0], moe_router[0])
    tok_of, tile_expert, n_used, pos1, pos2 = _moe_plan(idx, counts, min(MOE_TM, 2 * t))
    xs = _gather_norm(h3, moe_norm_g[0], tok_of)
    ha = _moe_gu(xs, moe_w_gu_b.reshape(n_exp, d, e_gu), tile_expert, n_used, tn=1024)
    ys = _moe_down(ha, moe_w_down_b.reshape(n_exp, -1, d), tile_expert, n_used, tn=1024)
    out = _moe_combine(h3, gates, ys, pos1, pos2)
    return out.reshape(b, s, d)
```

```python
import functools
import math

import jax
import jax.numpy as jnp
from jax import lax
from jax.experimental import pallas as pl
from jax.experimental.pallas import tpu as pltpu

F32 = jnp.float32
BF16 = jnp.bfloat16

RET_QK_DIM = 256
RET_V_DIM = 512
ROPE_BASE = 10000.0
ATT_HEAD_DIM = 128
MOBA_BLOCK = 256
MOBA_TOPK = 3
REL_BUCKETS = 32
REL_MAX_DISTANCE = 4096
N_EXPERTS = 8
EPS = 1e-6

LANES = 128
BF16_SUBLANES = 16
V7X_VMEM_LIMIT_BYTES = 56 * 1024 * 1024
MASK_VALUE = -1e30

LOG2E = math.log2(math.e)

RET_CHUNK = 512
ATT_KEYS_PER_STEP = 1024
ATT_Q_PER_STEP = 1024
BIAS_ZERO = ATT_KEYS_PER_STEP - LANES
ONES_ROWS = 16
MOE_TM = 512
GATHER_ROWS = 512
COMBINE_ROWS = 256
GATHER_UNROLL = 8


def _cparams(sem):
    return pltpu.CompilerParams(dimension_semantics=sem, vmem_limit_bytes=V7X_VMEM_LIMIT_BYTES)


def _sigmoid(x):
    return 1.0 / (1.0 + jnp.exp(-x))


def _rmsnorm_kernel(x_ref, *refs):
    n = len(refs) // 2
    x = x_ref[...]
    y = x * lax.rsqrt(jnp.mean(x * x, axis=-1, keepdims=True) + EPS)
    for g_ref, o_ref in zip(refs[:n], refs[n:]):
        o_ref[...] = (y * g_ref[...]).astype(o_ref.dtype)


def _rmsnorm(x, gains, name):
    t, d = x.shape
    tm = min(256, t)
    n = len(gains)
    row = pl.BlockSpec((tm, d), lambda i: (i, 0))
    gspec = pl.BlockSpec((1, d), lambda i: (0, 0))
    outs = pl.pallas_call(
        _rmsnorm_kernel,
        out_shape=[jax.ShapeDtypeStruct((t, d), BF16)] * n,
        grid=(t // tm,),
        in_specs=[row] + [gspec] * n,
        out_specs=[row] * n,
        compiler_params=_cparams(("arbitrary",)),
        name=name,
    )(x, *[g.reshape(1, d).astype(F32) for g in gains])
    return outs


def _cast_riders(riders, grid):
    steps = grid[0] * grid[1]
    specs, shapes = [], []
    for a in riders:
        rows, cols = a.shape
        nblk = next(n for n in range(min(steps, rows // BF16_SUBLANES), 0, -1)
                    if rows % n == 0 and (rows // n) % BF16_SUBLANES == 0)
        specs.append(pl.BlockSpec((rows // nblk, cols),
                                  lambda i, j, nblk=nblk: (jnp.minimum(i * grid[1] + j, nblk - 1), 0)))
        shapes.append(jax.ShapeDtypeStruct((rows, cols), BF16))
    return specs, shapes


def _run_riders(ins, outs):
    for src, dst in zip(ins, outs):
        dst[...] = src[...].astype(dst.dtype)


def _mm_kernel(x_ref, w_ref, *refs, mode, scale, n_riders):
    n_extra = 0 if mode == "plain" else 1
    extra = refs[:n_extra]
    o_ref = refs[n_extra + n_riders]
    _run_riders(refs[n_extra:n_extra + n_riders], refs[n_extra + n_riders + 1:])
    acc = jnp.dot(x_ref[...], w_ref[...], preferred_element_type=F32)
    if mode == "plain":
        o_ref[...] = acc.astype(o_ref.dtype)
    elif mode == "res":
        o_ref[...] = extra[0][...] + acc
    elif mode == "headnorm":
        g = extra[0][...] * scale
        for c in range(acc.shape[1] // LANES):
            a = acc[:, c * LANES:(c + 1) * LANES]
            y = a * lax.rsqrt(jnp.mean(a * a, axis=-1, keepdims=True) + EPS)
            o_ref[:, c * LANES:(c + 1) * LANES] = (y * g).astype(o_ref.dtype)
    else:
        raise ValueError(mode)


def _matmul(x, w, *, tm, tn, out_dtype, name, mode="plain", res=None, gain=None, scale=1.0, x_buffers=2,
            n_out=None, col0=0, riders=()):
    m, k = x.shape
    n = w.shape[1] if n_out is None else n_out
    tm, tn = min(tm, m), min(tn, n)
    assert m % tm == 0 and n % tn == 0 and col0 % tn == 0, (m, n, tm, tn, col0)
    grid = (m // tm, n // tn)
    jb = col0 // tn
    xspec = pl.BlockSpec((tm, k), lambda i, j: (i, 0), pipeline_mode=pl.Buffered(x_buffers))
    in_specs = [xspec, pl.BlockSpec((k, tn), lambda i, j: (0, j + jb))]
    args = [x, w]
    if mode == "res":
        in_specs.append(pl.BlockSpec((tm, tn), lambda i, j: (i, j)))
        args.append(res)
    elif mode == "headnorm":
        in_specs.append(pl.BlockSpec((1, LANES), lambda i, j: (0, 0)))
        args.append(gain.reshape(1, LANES).astype(F32))
    rider_specs, rider_shapes = _cast_riders(riders, grid)
    outs = pl.pallas_call(
        functools.partial(_mm_kernel, mode=mode, scale=scale, n_riders=len(riders)),
        out_shape=[jax.ShapeDtypeStruct((m, n), out_dtype)] + rider_shapes,
        grid=grid,
        in_specs=in_specs + rider_specs,
        out_specs=[pl.BlockSpec((tm, tn), lambda i, j: (i, j))] + rider_specs,
        compiler_params=_cparams(("arbitrary", "arbitrary")),
        name=name,
    )(*args, *riders)
    return outs[0], outs[1:]


def _swiglu_kernel(x_ref, wg_ref, wu_ref, *refs, n_riders):
    o_ref = refs[n_riders]
    _run_riders(refs[:n_riders], refs[n_riders + 1:])
    x = x_ref[...]
    g = jnp.dot(x, wg_ref[...], preferred_element_type=F32)
    u = jnp.dot(x, wu_ref[...], preferred_element_type=F32)
    o_ref[...] = (g * _sigmoid(g) * u).astype(o_ref.dtype)


def _swiglu(x, w_gu, *, tm, tn, name, riders=()):
    m, k = x.shape
    f = w_gu.shape[1] // 2
    tm, tn = min(tm, m), min(tn, f)
    assert m % tm == 0 and f % tn == 0
    nj = f // tn
    grid = (m // tm, nj)
    rider_specs, rider_shapes = _cast_riders(riders, grid)
    outs = pl.pallas_call(
        functools.partial(_swiglu_kernel, n_riders=len(riders)),
        out_shape=[jax.ShapeDtypeStruct((m, f), BF16)] + rider_shapes,
        grid=grid,
        in_specs=[
            pl.BlockSpec((tm, k), lambda i, j: (i, 0)),
            pl.BlockSpec((k, tn), lambda i, j: (0, j)),
            pl.BlockSpec((k, tn), lambda i, j: (0, j + nj)),
        ] + rider_specs,
        out_specs=[pl.BlockSpec((tm, tn), lambda i, j: (i, j))] + rider_specs,
        compiler_params=_cparams(("arbitrary", "arbitrary")),
        name=name,
    )(x, w_gu, w_gu, *riders)
    return outs[0], outs[1:]


def _retention_kernel(lg_ref, q_ref, k_ref, v_ref, g_ref, cos_ref, sin_ref, gn_ref, o_ref, state_ref, intra_ref,
                      dq_ref, dk_ref, *, chunk):
    h = pl.program_id(1)
    c = pl.program_id(2)
    lg = lg_ref[h]

    @pl.when(c == 0)
    def _():
        state_ref[...] = jnp.zeros_like(state_ref)
        row = lax.broadcasted_iota(jnp.int32, (chunk, 1), 0).astype(F32)
        col = lax.broadcasted_iota(jnp.int32, (1, chunk), 1).astype(F32)
        diff = row - col
        intra_ref[...] = jnp.where(diff >= 0, jnp.exp(lg * jnp.maximum(diff, 0.0)), 0.0)
        dq_ref[...] = jnp.broadcast_to(jnp.exp(lg * (row + 1.0)), dq_ref.shape)
        dk_ref[...] = jnp.broadcast_to(jnp.exp(lg * (chunk - 1.0 - row)), dk_ref.shape)

    cos = cos_ref[...]
    sin = sin_ref[...]
    half = RET_QK_DIM // 2

    def rot(x):
        x1, x2 = x[:, :half], x[:, half:]
        return jnp.concatenate([x1 * cos - x2 * sin, x1 * sin + x2 * cos], axis=-1)

    q = rot(q_ref[...].astype(F32))
    k = rot(k_ref[...].astype(F32)) * (RET_QK_DIM ** -0.5)
    v = v_ref[...]

    decay_chunk = jnp.exp(jnp.full((1, 1), chunk, F32) * lg)

    qb = q.astype(BF16)
    scores = lax.dot_general(qb, k.astype(BF16), (((1,), (1,)), ((), ())), preferred_element_type=F32)
    scores = scores * intra_ref[...]
    inner = jnp.dot(scores.astype(BF16), v, preferred_element_type=F32)
    state = state_ref[...]
    decay_q = jnp.concatenate([dq_ref[...]] * (RET_V_DIM // LANES), axis=-1)
    cross = jnp.dot(qb, state.astype(BF16), preferred_element_type=F32) * decay_q
    kd = (k * jnp.concatenate([dk_ref[...]] * (RET_QK_DIM // LANES), axis=-1)).astype(BF16)
    state_ref[...] = state * decay_chunk + lax.dot_general(
        kd, v, (((0,), (0,)), ((), ())), preferred_element_type=F32)

    o = inner + cross
    mu = jnp.mean(o, axis=-1, keepdims=True)
    oc = o - mu
    var = jnp.mean(oc * oc, axis=-1, keepdims=True)
    on = oc * lax.rsqrt(var + EPS) * gn_ref[...]
    gate = g_ref[...].astype(F32)
    o_ref[...] = (gate * _sigmoid(gate) * on).astype(o_ref.dtype)


def _retention(proj, gn_g, b, s, heads):
    dk, dv = RET_QK_DIM, RET_V_DIM
    chunk = min(RET_CHUNK, s)
    assert s % chunk == 0
    pos = jnp.arange(s, dtype=F32)
    inv = 1.0 / (ROPE_BASE ** jnp.linspace(0.0, 1.0, dk // 2, dtype=F32))
    ang = pos[:, None] * inv[None, :]
    cos, sin = jnp.cos(ang), jnp.sin(ang)
    log_gamma = jnp.log1p(-jnp.exp2(-5.0 - jnp.arange(heads, dtype=F32)))
    k_off = heads
    v_off = (2 * heads * dk) // dv
    g_off = v_off + heads
    grid_spec = pltpu.PrefetchScalarGridSpec(
        num_scalar_prefetch=1,
        grid=(b, heads, s // chunk),
        in_specs=[
            pl.BlockSpec((None, chunk, dk), lambda bi, h, c, lg: (bi, c, h)),
            pl.BlockSpec((None, chunk, dk), lambda bi, h, c, lg: (bi, c, k_off + h)),
            pl.BlockSpec((None, chunk, dv), lambda bi, h, c, lg: (bi, c, v_off + h)),
            pl.BlockSpec((None, chunk, dv), lambda bi, h, c, lg: (bi, c, g_off + h)),
            pl.BlockSpec((chunk, dk // 2), lambda bi, h, c, lg: (c, 0)),
            pl.BlockSpec((chunk, dk // 2), lambda bi, h, c, lg: (c, 0)),
            pl.BlockSpec((1, dv), lambda bi, h, c, lg: (0, h)),
        ],
        out_specs=pl.BlockSpec((None, chunk, dv), lambda bi, h, c, lg: (bi, c, h)),
        scratch_shapes=[
            pltpu.VMEM((dk, dv), F32),
            pltpu.VMEM((chunk, chunk), F32),
            pltpu.VMEM((chunk, LANES), F32),
            pltpu.VMEM((chunk, LANES), F32),
        ],
    )
    return pl.pallas_call(
        functools.partial(_retention_kernel, chunk=chunk),
        out_shape=jax.ShapeDtypeStruct((b, s, heads * dv), BF16),
        grid_spec=grid_spec,
        compiler_params=_cparams(("arbitrary", "arbitrary", "arbitrary")),
        name="retention",
    )(log_gamma, proj, proj, proj, proj, cos, sin, gn_g.reshape(1, heads * dv).astype(F32))


def _rel_bucket(dist):
    n = jnp.maximum(dist, 0)
    max_exact = REL_BUCKETS // 2
    nf = jnp.maximum(n, max_exact).astype(F32)
    large = max_exact + (jnp.log(nf / max_exact) / math.log(REL_MAX_DISTANCE / max_exact)
                         * (REL_BUCKETS - max_exact)).astype(jnp.int32)
    large = jnp.minimum(large, REL_BUCKETS - 1)
    return jnp.where(n < max_exact, n, large)


def _bias_windows(rel_bias, s):
    width = s + BIAS_ZERO
    heads = rel_bias.shape[1]
    u = jnp.arange(width + LANES)
    dist = u - (BIAS_ZERO + LANES - 1)
    tab = jnp.where(dist[None, :] >= 0, LOG2E * rel_bias.astype(F32).T[:, _rel_bucket(dist)], MASK_VALUE)
    return pl.pallas_call(
        _bias_window_kernel,
        out_shape=jax.ShapeDtypeStruct((heads, LANES, width), F32),
        grid=(heads,),
        in_specs=[pl.BlockSpec((None, 1, width + LANES), lambda h: (h, 0, 0))],
        out_specs=pl.BlockSpec((None, LANES, width), lambda h: (h, 0, 0)),
        compiler_params=_cparams(("arbitrary",)),
        name="bias_windows",
    )(tab.reshape(heads, 1, width + LANES))


def _bias_window_kernel(tab_ref, o_ref):
    w = tab_ref.shape[-1]
    x = jnp.broadcast_to(tab_ref[...], (LANES, w))
    shifted = pltpu.roll(x, w - (LANES - 1), 1, stride=1, stride_axis=0)
    o_ref[...] = shifted[:, :o_ref.shape[-1]]


def _moba_kernel(q_ref, k_ref, v_ref, bias_ref, o_ref, ka_ref, vt_ref, km_ref, s_ref, *, seq):
    blk, dh, gk = MOBA_BLOCK, ATT_HEAD_DIM, ATT_KEYS_PER_STEP
    nb = seq // blk
    nq = q_ref.shape[0]
    q0 = pl.program_id(2) * nq

    @pl.when(q0 == 0)
    def _():
        km_ref[...] = jnp.zeros_like(km_ref)
        lane = lax.broadcasted_iota(jnp.int32, (blk, LANES), 1)

        def fill(n, carry):
            rows = pl.ds(pl.multiple_of(n * blk, blk), blk)
            kb = k_ref[rows, :]
            km_ref[pl.ds(n, 1), :] = jnp.mean(kb.astype(F32), axis=0, keepdims=True)
            ka_ref[rows, 0:dh] = kb
            ka_ref[rows, dh:dh + LANES] = (lane == n).astype(BF16)
            vt_ref[0:dh, rows] = v_ref[rows, :].astype(F32).T.astype(BF16)
            vt_ref[dh:dh + ONES_ROWS, rows] = jnp.ones((ONES_ROWS, blk), BF16)
            return carry

        lax.fori_loop(0, nb, fill, 0)

    qt = q_ref[...].astype(F32).T.astype(BF16)

    km = km_ref[...]
    km_hi = km.astype(BF16)
    km_lo = (km - km_hi.astype(F32)).astype(BF16)
    gate = (jnp.dot(km_hi, qt, preferred_element_type=F32)
            + jnp.dot(km_lo, qt, preferred_element_type=F32))
    bid = lax.broadcasted_iota(jnp.int32, (LANES, nq), 0)
    own = (q0 + lax.broadcasted_iota(jnp.int32, (LANES, nq), 1)) // blk
    past = bid < own
    gate = jnp.where(past, gate, -jnp.inf)
    chosen = bid == own
    for _ in range(MOBA_TOPK):
        top = jnp.max(gate, axis=0, keepdims=True)
        idx = jnp.min(jnp.where(gate == top, bid, LANES), axis=0, keepdims=True)
        pick = bid == idx
        chosen = chosen | (pick & past)
        gate = jnp.where(pick, -jnp.inf, gate)
    penalty = jnp.where(chosen, 0.0, MASK_VALUE).astype(BF16)
    qa = jnp.concatenate([qt, penalty], axis=0)

    halves = 4
    hk = gk // halves

    def score(g):
        k0 = g * gk
        m0 = BIAS_ZERO + q0 - k0
        top = None
        for t in range(halves):
            keys = pl.ds(pl.multiple_of(k0 + t * hk, hk), hk)
            bias = jnp.concatenate(
                [bias_ref[:, pl.ds(pl.multiple_of(m0 - t * hk - LANES * a, LANES), nq)]
                 for a in range(hk // LANES)], axis=0)
            st = jnp.dot(ka_ref[keys, :], qa, preferred_element_type=F32) + bias
            s_ref[t * hk:(t + 1) * hk, :] = st
            mt = jnp.max(st, axis=0, keepdims=True)
            top = mt if top is None else jnp.maximum(top, mt)
        return top

    def absorb(g, top, m_prev, acc):
        m_new = jnp.maximum(m_prev, top)
        acc = jnp.exp2(m_prev - m_new) * acc
        for t in range(halves):
            keys = pl.ds(pl.multiple_of(g * gk + t * hk, hk), hk)
            p = jnp.exp2(s_ref[t * hk:(t + 1) * hk, :] - m_new)
            acc = acc + jnp.dot(vt_ref[:, keys], p.astype(BF16), preferred_element_type=F32)
        return m_new, acc

    def step(g, carry):
        top, m_prev, acc = carry
        m_new, acc = absorb(g, top, m_prev, acc)
        return score(g + 1), m_new, acc

    last = q0 // gk
    init = (score(0), jnp.full((1, nq), MASK_VALUE, F32), jnp.zeros((dh + ONES_ROWS, nq), F32))
    top, m_prev, acc = lax.fori_loop(0, last, step, init)
    _, acc = absorb(last, top, m_prev, acc)
    o_ref[...] = (acc[0:dh, :] / acc[dh:dh + 1, :]).T.astype(o_ref.dtype)


def _moba_attention(q, k, v, bias_win, b, s, heads):
    dh, blk = ATT_HEAD_DIM, MOBA_BLOCK
    nq = min(ATT_Q_PER_STEP, s)
    assert dh == LANES and s % ATT_KEYS_PER_STEP == 0 and s // blk <= LANES
    assert ATT_KEYS_PER_STEP % nq == 0 and nq % blk == 0
    qspec = pl.BlockSpec((None, nq, dh), lambda bi, h, i: (bi, i, h))
    kvspec = pl.BlockSpec((None, s, dh), lambda bi, h, i: (bi, 0, h))
    return pl.pallas_call(
        functools.partial(_moba_kernel, seq=s),
        out_shape=jax.ShapeDtypeStruct((b, s, heads * dh), BF16),
        grid=(b, heads, s // nq),
        in_specs=[qspec, kvspec, kvspec,
                  pl.BlockSpec((None, LANES, s + BIAS_ZERO), lambda bi, h, i: (h, 0, 0))],
        out_specs=qspec,
        scratch_shapes=[
            pltpu.VMEM((s, dh + LANES), BF16),
            pltpu.VMEM((dh + ONES_ROWS, s), BF16),
            pltpu.VMEM((LANES, dh), F32),
            pltpu.VMEM((ATT_KEYS_PER_STEP, nq), F32),
        ],
        compiler_params=_cparams(("arbitrary", "arbitrary", "arbitrary")),
        name="moba_attention",
    )(q, k, v, bias_win)


def _router_kernel(x_ref, g_ref, rw_ref, gates_ref, idx_ref, cnt_ref):
    @pl.when(pl.program_id(0) == 0)
    def _():
        cnt_ref[...] = jnp.zeros_like(cnt_ref)

    x = x_ref[...]
    xn = x * lax.rsqrt(jnp.mean(x * x, axis=-1, keepdims=True) + EPS) * g_ref[...]
    xn_hi = xn.astype(BF16)
    xn_lo = (xn - xn_hi.astype(F32)).astype(BF16)
    rw = rw_ref[...]
    both = jnp.dot(xn_hi, rw, preferred_element_type=F32)
    logits = (both[:, :LANES] + both[:, LANES:]) + jnp.dot(xn_lo, rw[:, :LANES], preferred_element_type=F32)
    lane = lax.broadcasted_iota(jnp.int32, logits.shape, 1)
    logits = jnp.where(lane < N_EXPERTS, logits, -jnp.inf)
    v1 = jnp.max(logits, axis=-1, keepdims=True)
    i1 = jnp.min(jnp.where(logits == v1, lane, LANES), axis=-1, keepdims=True)
    rest = jnp.where(lane == i1, -jnp.inf, logits)
    v2 = jnp.max(rest, axis=-1, keepdims=True)
    i2 = jnp.min(jnp.where(rest == v2, lane, LANES), axis=-1, keepdims=True)
    e2 = jnp.exp(v2 - v1)
    w1 = 1.0 / (1.0 + e2)
    w2 = e2 / (1.0 + e2)
    gates_ref[...] = jnp.where(lane == 0, w1, jnp.where(lane == 1, w2, 0.0))

    tm = x.shape[0]
    picked = ((lane == i1) | (lane == i2)).astype(BF16)
    before = (lax.broadcasted_iota(jnp.int32, (tm, tm), 1)
              < lax.broadcasted_iota(jnp.int32, (tm, tm), 0)).astype(BF16)
    prior = jnp.dot(before, picked, preferred_element_type=F32) + cnt_ref[...]
    r1 = jnp.sum(jnp.where(lane == i1, prior, 0.0), axis=-1, keepdims=True).astype(jnp.int32)
    r2 = jnp.sum(jnp.where(lane == i2, prior, 0.0), axis=-1, keepdims=True).astype(jnp.int32)
    cnt_ref[...] += jnp.sum(picked.astype(F32), axis=0, keepdims=True)
    packed = jnp.where(lane == 3, r2, 0)
    for at, val in ((2, r1), (1, i2), (0, i1)):
        packed = jnp.where(lane == at, val, packed)
    idx_ref[...] = packed


def _router(h, norm_g, router_w):
    t, d = h.shape
    tm = min(256, t)
    rw = jnp.zeros((d, LANES), F32).at[:, :N_EXPERTS].set(router_w.astype(F32))
    rw_hi = rw.astype(BF16)
    rw = jnp.concatenate([rw_hi, (rw - rw_hi.astype(F32)).astype(BF16)], axis=1)
    row = pl.BlockSpec((tm, d), lambda i: (i, 0))
    out = pl.BlockSpec((tm, LANES), lambda i: (i, 0))
    return pl.pallas_call(
        _router_kernel,
        out_shape=[jax.ShapeDtypeStruct((t, LANES), F32), jax.ShapeDtypeStruct((t, LANES), jnp.int32),
                   jax.ShapeDtypeStruct((1, LANES), F32)],
        grid=(t // tm,),
        in_specs=[row, pl.BlockSpec((1, d), lambda i: (0, 0)), pl.BlockSpec((d, 2 * LANES), lambda i: (0, 0))],
        out_specs=[out, out, pl.BlockSpec((1, LANES), lambda i: (0, 0))],
        compiler_params=_cparams(("arbitrary",)),
        name="moe_router",
    )(h, norm_g.reshape(1, d).astype(F32), rw)


def _row_copy(src_hbm, dst, sem, src_row, dst_row):
    return pltpu.make_async_copy(src_hbm.at[pl.ds(src_row, 1), :], dst.at[pl.ds(dst_row, 1), :], sem)


def _start_row_gather(idx_ref, src_hbm, dst, sem, rows):
    def start(r, carry):
        _row_copy(src_hbm, dst, sem, idx_ref[0, r], r).start()
        return carry

    lax.fori_loop(0, rows, start, 0, unroll=GATHER_UNROLL)


def _wait_row_gather(src_hbm, dst, sem, rows):
    def wait(r, carry):
        _row_copy(src_hbm, dst, sem, 0, r).wait()
        return carry

    lax.fori_loop(0, rows, wait, 0, unroll=GATHER_UNROLL)


def _prefetched_row_gather(idx_refs, nxt_refs, src_hbm, bufs, sems, rows):
    i = pl.program_id(0)
    slot = i % 2

    @pl.when(i == 0)
    def _():
        for idx_ref, buf, sem in zip(idx_refs, bufs, sems):
            _start_row_gather(idx_ref, src_hbm, buf.at[0], sem.at[0], rows)

    @pl.when(i + 1 < pl.num_programs(0))
    def _():
        for nxt_ref, buf, sem in zip(nxt_refs, bufs, sems):
            _start_row_gather(nxt_ref, src_hbm, buf.at[1 - slot], sem.at[1 - slot], rows)

    for buf, sem in zip(bufs, sems):
        _wait_row_gather(src_hbm, buf.at[slot], sem.at[slot], rows)
    return slot


def _idx_specs(n_steps, rows):
    cur = pl.BlockSpec((None, 1, rows), lambda i: (i, 0, 0), memory_space=pltpu.SMEM)
    nxt = pl.BlockSpec((None, 1, rows), lambda i: (jnp.minimum(i + 1, n_steps - 1), 0, 0),
                       memory_space=pltpu.SMEM)
    return cur, nxt


def _gather_norm_kernel(tok_ref, nxt_ref, h_hbm, g_ref, o_ref, buf, sem, *, rows):
    slot = _prefetched_row_gather([tok_ref], [nxt_ref], h_hbm, [buf], [sem], rows)
    g = g_ref[...]

    def norm(c, carry):
        r = pl.ds(pl.multiple_of(c * BF16_SUBLANES, BF16_SUBLANES), BF16_SUBLANES)
        x = buf[slot, r, :]
        y = x * lax.rsqrt(jnp.mean(x * x, axis=-1, keepdims=True) + EPS)
        o_ref[r, :] = (y * g).astype(o_ref.dtype)
        return carry

    lax.fori_loop(0, rows // BF16_SUBLANES, norm, 0, unroll=4)


def _gather_norm(h, norm_g, tok_of):
    t, d = h.shape
    p = tok_of.shape[0]
    rows = min(GATHER_ROWS, p)
    assert p % rows == 0
    n_steps = p // rows
    cur, nxt = _idx_specs(n_steps, rows)
    tok3 = tok_of.reshape(n_steps, 1, rows)
    return pl.pallas_call(
        functools.partial(_gather_norm_kernel, rows=rows),
        out_shape=jax.ShapeDtypeStruct((p, d), BF16),
        grid=(n_steps,),
        in_specs=[cur, nxt, pl.BlockSpec(memory_space=pl.ANY), pl.BlockSpec((1, d), lambda i: (0, 0))],
        out_specs=pl.BlockSpec((rows, d), lambda i: (i, 0)),
        scratch_shapes=[pltpu.VMEM((2, rows, d), F32), pltpu.SemaphoreType.DMA((2,))],
        compiler_params=_cparams(("arbitrary",)),
        name="moe_gather_norm",
    )(tok3, tok3, h, norm_g.reshape(1, d).astype(F32))


def _moe_gu_kernel(te_ref, nu_ref, x_ref, wg_ref, wu_ref, o_ref):
    @pl.when(pl.program_id(0) < nu_ref[0])
    def _():
        x = x_ref[...]
        g = jnp.dot(x, wg_ref[...], preferred_element_type=F32)
        u = jnp.dot(x, wu_ref[...], preferred_element_type=F32)
        o_ref[...] = (g * _sigmoid(g) * u).astype(o_ref.dtype)

    @pl.when(pl.program_id(0) >= nu_ref[0])
    def _():
        o_ref[...] = jnp.zeros_like(o_ref)


def _moe_gu(xs, w_gu, tile_expert, n_used, *, tn):
    p, k = xs.shape
    f = w_gu.shape[2] // 2
    tm = min(MOE_TM, p)
    tn = min(tn, f)
    nj = f // tn

    def wmap(off):
        return lambda t, j, te, nu: (te[t], 0, jnp.where(t < nu[0], j, 0) + off)

    grid_spec = pltpu.PrefetchScalarGridSpec(
        num_scalar_prefetch=2,
        grid=(p // tm, nj),
        in_specs=[
            pl.BlockSpec((tm, k), lambda t, j, te, nu: (t, 0)),
            pl.BlockSpec((None, k, tn), wmap(0)),
            pl.BlockSpec((None, k, tn), wmap(nj)),
        ],
        out_specs=pl.BlockSpec((tm, tn), lambda t, j, te, nu: (t, j)),
    )
    return pl.pallas_call(
        _moe_gu_kernel,
        out_shape=jax.ShapeDtypeStruct((p, f), BF16),
        grid_spec=grid_spec,
        compiler_params=_cparams(("arbitrary", "arbitrary")),
        name="moe_gate_up",
    )(tile_expert, n_used, xs, w_gu, w_gu)


def _moe_down_kernel(te_ref, nu_ref, x_ref, w_ref, o_ref):
    @pl.when(pl.program_id(0) < nu_ref[0])
    def _():
        o_ref[...] = jnp.dot(x_ref[...], w_ref[...], preferred_element_type=F32)

    @pl.when(pl.program_id(0) >= nu_ref[0])
    def _():
        o_ref[...] = jnp.zeros_like(o_ref)


def _moe_down(ha, w_down, tile_expert, n_used, *, tn):
    p, k = ha.shape
    n = w_down.shape[2]
    tm = min(MOE_TM, p)
    tn = min(tn, n)
    grid_spec = pltpu.PrefetchScalarGridSpec(
        num_scalar_prefetch=2,
        grid=(p // tm, n // tn),
        in_specs=[
            pl.BlockSpec((tm, k), lambda t, j, te, nu: (t, 0)),
            pl.BlockSpec((None, k, tn), lambda t, j, te, nu: (te[t], 0, jnp.where(t < nu[0], j, 0))),
        ],
        out_specs=pl.BlockSpec((tm, tn), lambda t, j, te, nu: (t, j)),
    )
    return pl.pallas_call(
        _moe_down_kernel,
        out_shape=jax.ShapeDtypeStruct((p, n), F32),
        grid_spec=grid_spec,
        compiler_params=_cparams(("arbitrary", "arbitrary")),
        name="moe_down",
    )(tile_expert, n_used, ha, w_down)


def _combine_kernel(p1_ref, n1_ref, p2_ref, n2_ref, h_ref, gates_ref, ys_hbm, o_ref, buf1, buf2, sem1, sem2, *,
                    rows):
    slot = _prefetched_row_gather([p1_ref, p2_ref], [n1_ref, n2_ref], ys_hbm, [buf1, buf2], [sem1, sem2], rows)
    gates = gates_ref[...]
    o_ref[...] = h_ref[...] + (gates[:, 0:1] * buf1[slot] + gates[:, 1:2] * buf2[slot])


def _moe_combine(h, gates, ys, pos1, pos2):
    t, d = h.shape
    rows = min(COMBINE_ROWS, t)
    n_steps = t // rows
    cur, nxt = _idx_specs(n_steps, rows)
    row = pl.BlockSpec((rows, d), lambda i: (i, 0))
    p1 = pos1.reshape(n_steps, 1, rows)
    p2 = pos2.reshape(n_steps, 1, rows)
    return pl.pallas_call(
        functools.partial(_combine_kernel, rows=rows),
        out_shape=jax.ShapeDtypeStruct((t, d), F32),
        grid=(n_steps,),
        in_specs=[cur, nxt, cur, nxt, row, pl.BlockSpec((rows, LANES), lambda i: (i, 0)),
                  pl.BlockSpec(memory_space=pl.ANY)],
        out_specs=row,
        scratch_shapes=[pltpu.VMEM((2, rows, d), F32), pltpu.VMEM((2, rows, d), F32),
                        pltpu.SemaphoreType.DMA((2,)), pltpu.SemaphoreType.DMA((2,))],
        compiler_params=_cparams(("arbitrary",)),
        name="moe_combine",
    )(p1, p1, p2, p2, h, gates, ys)


def _moe_plan(idx, counts, tm):
    t = idx.shape[0]
    counts = counts[0, :N_EXPERTS].astype(jnp.int32)
    padded = ((counts + tm - 1) // tm) * tm
    ends = jnp.cumsum(padded)
    starts = ends - padded
    experts = jnp.arange(N_EXPERTS, dtype=jnp.int32)[None, :]
    pos1 = jnp.sum(jnp.where(idx[:, 0:1] == experts, starts[None, :], 0), axis=1) + idx[:, 2]
    pos2 = jnp.sum(jnp.where(idx[:, 1:2] == experts, starts[None, :], 0), axis=1) + idx[:, 3]
    p_rows = 2 * t + N_EXPERTS * tm
    tok = jnp.arange(t, dtype=jnp.int32)
    tok_of = jnp.zeros((p_rows,), jnp.int32).at[jnp.concatenate([pos1, pos2])].set(jnp.concatenate([tok, tok]))
    n_tiles = p_rows // tm
    n_used = (ends[-1] // tm).astype(jnp.int32)
    tile_start = jnp.arange(n_tiles, dtype=jnp.int32) * tm
    tile_expert = jnp.sum((tile_start[:, None] >= ends[None, :]).astype(jnp.int32), axis=1)
    last_expert = jnp.sum(((ends[-1] - 1) >= ends).astype(jnp.int32))
    tile_expert = jnp.where(tile_start < ends[-1], tile_expert, last_expert).astype(jnp.int32)
    return tok_of, tile_expert, n_used.reshape(1), pos1.astype(jnp.int32), pos2.astype(jnp.int32)


def kernel(x, ret_norm_g, ret_w_in, ret_gn_g, ret_w_out, kv_norm_g, kv_w, k_norm_g, rel_bias, att_norm_g, att_w_q, q_norm_g, att_w_out, ffn_norm_g, ffn_w_gu, ffn_w_down, moe_norm_g, moe_router, moe_w_gu, moe_w_down):
    b, s, d = x.shape
    t = b * s
    ret_heads = d // RET_QK_DIM
    att_heads = d // ATT_HEAD_DIM
    att_dim = att_heads * ATT_HEAD_DIM
    h0 = x.reshape(t, d)

    n_exp, _, e_gu = moe_w_gu[0].shape

    (xn,) = _rmsnorm(h0, [ret_norm_g[0]], "ret_norm")
    proj, (ret_w_out_b, ffn_w_gu_b, ffn_w_down_b) = _matmul(
        xn, ret_w_in[0].astype(BF16), tm=1024, tn=1024, out_dtype=BF16, name="ret_in_proj",
        riders=(ret_w_out[0], ffn_w_gu[0], ffn_w_down[0]))
    y = _retention(proj.reshape(b, s, -1), ret_gn_g[0], b, s, ret_heads).reshape(t, -1)
    h1, (kv_w_b, att_w_q_b, att_w_out_b) = _matmul(
        y, ret_w_out_b, tm=1024, tn=512, out_dtype=F32, mode="res", res=h0, x_buffers=1, name="ret_out_proj",
        riders=(kv_w, att_w_q[0], att_w_out[0]))

    (xn,) = _rmsnorm(h1, [ffn_norm_g[0]], "ffn_norm")
    a, (moe_w_gu_b,) = _swiglu(xn, ffn_w_gu_b, tm=1024, tn=256, name="ffn_gate_up",
                               riders=(moe_w_gu[0].reshape(n_exp * d, e_gu),))
    h2, (moe_w_down_b,) = _matmul(a, ffn_w_down_b, tm=1024, tn=256, out_dtype=F32, mode="res", res=h1,
                                  x_buffers=1, name="ffn_down",
                                  riders=(moe_w_down[0].reshape(-1, d),))

    xkv, xq = _rmsnorm(h2, [kv_norm_g, att_norm_g[0]], "kv_att_norm")
    k, _ = _matmul(xkv, kv_w_b, tm=1024, tn=1024, out_dtype=BF16, mode="headnorm", gain=k_norm_g,
                   n_out=att_dim, name="k_proj")
    v, _ = _matmul(xkv, kv_w_b, tm=1024, tn=1024, out_dtype=BF16, n_out=att_dim, col0=att_dim, name="v_proj")
    q, _ = _matmul(xq, att_w_q_b, tm=1024, tn=1024, out_dtype=BF16, mode="headnorm",
                   gain=q_norm_g[0], scale=LOG2E * ATT_HEAD_DIM ** -0.5, name="q_proj")
    bias_win = _bias_windows(rel_bias, s)
    o = _moba_attention(q.reshape(b, s, att_dim), k.reshape(b, s, att_dim), v.reshape(b, s, att_dim),
                        bias_win, b, s, att_heads).reshape(t, att_dim)
    h3, _ = _matmul(o, att_w_out_b, tm=1024, tn=1024, out_dtype=F32, mode="res", res=h2, name="att_out_proj")

    gates, idx, counts = _router(h3, moe_norm_g[0], moe_router[0])
    tok_of, tile_expert, n_used, pos1, pos2 = _moe_plan(idx, counts, min(MOE_TM, 2 * t))
    xs = _gather_norm(h3, moe_norm_g[0], tok_of)
    ha = _moe_gu(xs, moe_w_gu_b.reshape(n_exp, d, e_gu), tile_expert, n_used, tn=1024)
    ys = _moe_down(ha, moe_w_down_b.reshape(n_exp, -1, d), tile_expert, n_used, tn=2048)
    out = _moe_combine(h3, gates, ys, pos1, pos2)
    return out.reshape(b, s, d)
```
